```python
import math
import jax, jax.numpy as jnp
from jax import lax
import numpy as np

D_MODEL = 1024
BATCH = 4
SEQ = 8192
DEPTH = 2

HG_HEADS = 4
HG_DK = 128
HG_DV = 128
HG_KW = HG_HEADS * HG_DK
HG_WIDTH = HG_HEADS * HG_DV
HG_CHUNK = 32
FX_HEADS = 4
FX_DH = 128
FX_WIDTH = FX_HEADS * FX_DH
Q_BLOCK = 128
ML_HEADS = 4
ML_DK = 128
ML_DV = 256
ML_KW = ML_HEADS * ML_DK
ML_WIDTH = ML_HEADS * ML_DV
ML_CONV = 4
ML_CHUNK = 64

N_AB = (DEPTH + 1) // 2
N_C = DEPTH // 2
ALPHA = (2 * DEPTH) ** 0.25
BETA = (8 * DEPTH) ** -0.25
LN_EPS = 1e-5
RMS_EPS = 1e-6

AB_SIZES = [HG_KW, HG_KW, HG_WIDTH, HG_WIDTH, FX_WIDTH, FX_WIDTH, FX_WIDTH, FX_HEADS, FX_WIDTH]
AB_VALUE_SLOTS = (2, 6)
AB_COLS = sum(AB_SIZES)
C_SIZES = [ML_KW, ML_KW, ML_WIDTH, ML_HEADS, ML_HEADS, ML_WIDTH, ML_WIDTH]
C_VALUE_SLOTS = (2,)
C_COLS = sum(C_SIZES)
AB_OUT_IN = HG_WIDTH + FX_WIDTH

kernel_name = "hybrid_hgrn2_fox_mlstm_deepnorm"


def _split(h, sizes):
    idx = np.cumsum(sizes)[:-1].tolist()
    return jnp.split(h, idx, axis=-1)


def _to_chunks(t, n_heads, d, L):
    B, S, _ = t.shape
    return t.reshape(B, S // L, L, n_heads, d).transpose(1, 0, 3, 2, 4)


def _gate_chunks(t, L):
    B, S, H = t.shape
    return t.reshape(B, S // L, L, H).transpose(1, 0, 3, 2)


def _from_chunks(t):
    NC, B, H, L, d = t.shape
    return t.transpose(1, 0, 3, 2, 4).reshape(B, NC * L, H * d)


def _rmsnorm_heads(t, n_heads, g):
    B, S, W = t.shape
    th = t.reshape(B, S, n_heads, W // n_heads)
    th = th * lax.rsqrt(jnp.mean(th * th, axis=-1, keepdims=True) + RMS_EPS)
    return th.reshape(B, S, W) * g.astype(jnp.float32)


def _layernorm(t, g, b):
    mu = jnp.mean(t, axis=-1, keepdims=True)
    var = jnp.mean(jnp.square(t - mu), axis=-1, keepdims=True)
    return (t - mu) * lax.rsqrt(var + LN_EPS) * g.astype(jnp.float32) + b.astype(jnp.float32)


def _hgrn2_mix(q, f_logit, v, lb):
    B = q.shape[0]
    f = lb + (1.0 - lb) * jax.nn.sigmoid(f_logit)
    log_f = jnp.log(f)
    k = 1.0 - f
    qc = _to_chunks(q, HG_HEADS, HG_DK, HG_CHUNK)
    kc = _to_chunks(k, HG_HEADS, HG_DK, HG_CHUNK)
    gc = _to_chunks(log_f, HG_HEADS, HG_DK, HG_CHUNK)
    vc = _to_chunks(v, HG_HEADS, HG_DV, HG_CHUNK)
    mask = jnp.tril(jnp.ones((HG_CHUNK, HG_CHUNK), dtype=bool))

    def step(state, inp):
        qb, kb, vb, gb = inp
        b = jnp.cumsum(gb, axis=2)
        q_dec = qb * jnp.exp(b)
        k_inv = kb * jnp.exp(-b)
        a = jnp.where(mask, jnp.einsum('bhtk,bhsk->bhts', q_dec, k_inv), 0.0)
        o = jnp.einsum('bhts,bhsv->bhtv', a, vb) + jnp.einsum('bhtk,bhkv->bhtv', q_dec, state)
        b_last = b[:, :, -1:, :]
        k_end = kb * jnp.exp(b_last - b)
        new_state = jnp.exp(b_last[:, :, 0, :])[..., None] * state + jnp.einsum('bhsk,bhsv->bhkv', k_end, vb)
        return new_state, o

    s0 = jnp.zeros((B, HG_HEADS, HG_DK, HG_DV), jnp.float32)
    _, o = lax.scan(step, s0, (qc, kc, vc, gc))
    return _from_chunks(o)


def _fox_mix(q, k, v, f_logit, b_f):
    B, S, _ = q.shape
    log_f = jax.nn.log_sigmoid(f_logit + b_f.astype(jnp.float32))
    c = jnp.cumsum(log_f, axis=1).transpose(0, 2, 1)
    scale = FX_DH ** -0.5
    qh = q.reshape(B, S, FX_HEADS, FX_DH).transpose(0, 2, 1, 3) * scale
    kh = k.reshape(B, S, FX_HEADS, FX_DH).transpose(0, 2, 1, 3)
    vh = v.reshape(B, S, FX_HEADS, FX_DH).transpose(0, 2, 1, 3)
    n_blk = S // Q_BLOCK
    q_blocks = qh.reshape(B, FX_HEADS, n_blk, Q_BLOCK, FX_DH).transpose(2, 0, 1, 3, 4)
    c_blocks = c.reshape(B, FX_HEADS, n_blk, Q_BLOCK).transpose(2, 0, 1, 3)
    key_pos = jnp.arange(S)

    def block(args):
        qb, cb, i = args
        s = jnp.einsum('bhtd,bhsd->bhts', qb, kh) + cb[..., :, None] - c[:, :, None, :]
        q_pos = i * Q_BLOCK + jnp.arange(Q_BLOCK)
        s = jnp.where(key_pos[None, :] <= q_pos[:, None], s, -jnp.inf)
        p = jax.nn.softmax(s, axis=-1)
        return jnp.einsum('bhts,bhsd->bhtd', p, vh)

    out = lax.map(block, (q_blocks, c_blocks, jnp.arange(n_blk)))
    return out.transpose(1, 2, 0, 3, 4).reshape(B, FX_HEADS, S, FX_DH).transpose(0, 2, 1, 3).reshape(B, S, FX_WIDTH)


def _causal_conv(u, w, b):
    C = u.shape[-1]
    out = lax.conv_general_dilated(u, w.astype(jnp.float32)[:, None, :], window_strides=(1,),
                                   padding=[(ML_CONV - 1, 0)], dimension_numbers=('NWC', 'WIO', 'NWC'),
                                   feature_group_count=C)
    return out + b.astype(jnp.float32)


def _mlstm_mix(q, k, v, i_pre, f_pre):
    B = q.shape[0]
    k = k * (ML_DK ** -0.5)
    qc = _to_chunks(q, ML_HEADS, ML_DK, ML_CHUNK)
    kc = _to_chunks(k, ML_HEADS, ML_DK, ML_CHUNK)
    vc = _to_chunks(v, ML_HEADS, ML_DV, ML_CHUNK)
    ic = _gate_chunks(i_pre, ML_CHUNK)
    gc = _gate_chunks(jax.nn.log_sigmoid(f_pre), ML_CHUNK)
    mask = jnp.tril(jnp.ones((ML_CHUNK, ML_CHUNK), dtype=bool))

    def step(carry, inp):
        Cm, n, m = carry
        qb, kb, vb, ib, gb = inp
        b = jnp.cumsum(gb, axis=-1)
        d = jnp.where(mask, b[..., :, None] - b[..., None, :] + ib[..., None, :], -jnp.inf)
        inter = b + m[..., None]
        m_t = jnp.maximum(inter, jnp.max(d, axis=-1))
        w = jnp.exp(d - m_t[..., None])
        s_inter = jnp.exp(inter - m_t)
        qk = jnp.einsum('bhtk,bhsk->bhts', qb, kb) * w
        num = jnp.einsum('bhts,bhsv->bhtv', qk, vb) + s_inter[..., None] * jnp.einsum('bhtk,bhkv->bhtv', qb, Cm)
        den = jnp.sum(qk, axis=-1) + s_inter * jnp.einsum('bhtk,bhk->bht', qb, n)
        h = num / jnp.maximum(jnp.abs(den), jnp.exp(-m_t))[..., None]
        b_last = b[..., -1]
        g_end = b_last[..., None] - b + ib
        m_new = jnp.maximum(b_last + m, jnp.max(g_end, axis=-1))
        wk = jnp.exp(g_end - m_new[..., None])
        decay = jnp.exp(b_last + m - m_new)
        C_new = decay[..., None, None] * Cm + jnp.einsum('bhs,bhsk,bhsv->bhkv', wk, kb, vb)
        n_new = decay[..., None] * n + jnp.einsum('bhs,bhsk->bhk', wk, kb)
        return (C_new, n_new, m_new), h

    carry0 = (jnp.zeros((B, ML_HEADS, ML_DK, ML_DV), jnp.float32),
              jnp.zeros((B, ML_HEADS, ML_DK), jnp.float32),
              jnp.zeros((B, ML_HEADS), jnp.float32))
    _, h = lax.scan(step, carry0, (qc, kc, vc, ic, gc))
    return _from_chunks(h)


def _ab_layer(x32, lb, w_in, fox_bf, hg_norm_g, w_out):
    h = jnp.einsum('bsd,dc->bsc', x32, w_in.astype(jnp.float32))
    hq, hf, hi, hz, fq, fk, fv, ff, fz = _split(h, AB_SIZES)
    hg = _hgrn2_mix(hq, hf, hi, lb)
    hg = _rmsnorm_heads(hg, HG_HEADS, hg_norm_g) * jax.nn.silu(hz)
    fx = _fox_mix(fq, fk, fv, ff, fox_bf) * jax.nn.silu(fz)
    return jnp.einsum('bsc,cd->bsd', jnp.concatenate([hg, fx], axis=-1), w_out.astype(jnp.float32))


def _c_layer(x32, w_in, conv_w, conv_b, b_i, b_f, norm_g, w_out):
    h = jnp.einsum('bsd,dc->bsc', x32, w_in.astype(jnp.float32))
    mq, mk, mv, mi, mf, mo, mz = _split(h, C_SIZES)
    qk = jax.nn.silu(_causal_conv(jnp.concatenate([mq, mk], axis=-1), conv_w, conv_b))
    mq, mk = jnp.split(qk, [ML_KW], axis=-1)
    ht = _mlstm_mix(mq, mk, mv, mi + b_i.astype(jnp.float32), mf + b_f.astype(jnp.float32))
    ht = jax.nn.sigmoid(mo) * ht
    ht = _rmsnorm_heads(ht, ML_HEADS, norm_g) * jax.nn.silu(mz)
    return jnp.einsum('bsc,cd->bsd', ht, w_out.astype(jnp.float32))


def setup_inputs(seed: int = 0) -> dict:
    key = jax.random.key(seed)
    ks = jax.random.split(key, 16)
    f32 = jnp.float32

    def col_scale(sizes, slots):
        return jnp.concatenate([jnp.full((s,), BETA if j in slots else 1.0, f32) for j, s in enumerate(sizes)])

    x = jax.random.normal(ks[0], (BATCH, SEQ, D_MODEL), f32)
    hgrn_lb_logits = 1.0 + 0.1 * jax.random.normal(ks[1], (DEPTH + 1, HG_KW), f32)
    ab_w_in = jax.random.normal(ks[2], (N_AB, D_MODEL, AB_COLS), f32) * (D_MODEL ** -0.5) * col_scale(AB_SIZES, AB_VALUE_SLOTS)
    ab_fox_bf = 0.1 * jax.random.normal(ks[3], (N_AB, FX_HEADS), f32)
    ab_hgrn_norm_g = 1.0 + 0.02 * jax.random.normal(ks[4], (N_AB, HG_WIDTH), f32)
    ab_w_out = jax.random.normal(ks[5], (N_AB, AB_OUT_IN, D_MODEL), f32) * (AB_OUT_IN ** -0.5) * BETA
    c_w_in = jax.random.normal(ks[6], (N_C, D_MODEL, C_COLS), f32) * (D_MODEL ** -0.5) * col_scale(C_SIZES, C_VALUE_SLOTS)
    c_conv_w = jax.random.normal(ks[7], (N_C, ML_CONV, 2 * ML_KW), f32) * (ML_CONV ** -0.5)
    c_conv_b = 0.02 * jax.random.normal(ks[8], (N_C, 2 * ML_KW), f32)
    c_bi = 0.1 * jax.random.normal(ks[9], (N_C, ML_HEADS), f32)
    c_bf = jax.random.uniform(ks[10], (N_C, ML_HEADS), f32, 3.0, 6.0)
    c_norm_g = 1.0 + 0.02 * jax.random.normal(ks[11], (N_C, ML_WIDTH), f32)
    c_w_out = jax.random.normal(ks[12], (N_C, ML_WIDTH, D_MODEL), f32) * (ML_WIDTH ** -0.5) * BETA
    ln_g = 1.0 + 0.02 * jax.random.normal(ks[13], (DEPTH, D_MODEL), f32)
    ln_b = 0.02 * jax.random.normal(ks[14], (DEPTH, D_MODEL), f32)
    return {"x": x, "hgrn_lb_logits": hgrn_lb_logits, "ab_w_in": ab_w_in, "ab_fox_bf": ab_fox_bf,
            "ab_hgrn_norm_g": ab_hgrn_norm_g, "ab_w_out": ab_w_out, "c_w_in": c_w_in,
            "c_conv_w": c_conv_w, "c_conv_b": c_conv_b, "c_bi": c_bi, "c_bf": c_bf,
            "c_norm_g": c_norm_g, "c_w_out": c_w_out, "ln_g": ln_g, "ln_b": ln_b}


def reference(x, hgrn_lb_logits, ab_w_in, ab_fox_bf, ab_hgrn_norm_g, ab_w_out, c_w_in,
              c_conv_w, c_conv_b, c_bi, c_bf, c_norm_g, c_w_out, ln_g, ln_b):
    lb_table = jnp.cumsum(jax.nn.softmax(hgrn_lb_logits.astype(jnp.float32), axis=0), axis=0)
    h = x.astype(jnp.float32)
    for l in range(DEPTH):
        if l % 2 == 0:
            j = l // 2
            y = _ab_layer(h, lb_table[l], ab_w_in[j], ab_fox_bf[j], ab_hgrn_norm_g[j], ab_w_out[j])
        else:
            j = l // 2
            y = _c_layer(h, c_w_in[j], c_conv_w[j], c_conv_b[j], c_bi[j], c_bf[j], c_norm_g[j], c_w_out[j])
        h = _layernorm(ALPHA * h + y, ln_g[l], ln_b[l])
    return h.astype(x.dtype)
```

```python
import functools
import math

import jax
import jax.numpy as jnp
from jax import lax
from jax.experimental import pallas as pl
from jax.experimental.pallas import tpu as pltpu

F32 = jnp.float32
BF16 = jnp.bfloat16

DEPTH = 2
ALPHA = (2 * DEPTH) ** 0.25
LN_EPS = 1e-5
RMS_EPS = 1e-6

HEADS = 4
HEAD_DK = 128
ML_DV = 256
HG_CHUNK = 32
ML_CONV = 4
GATE_ROWS = 8

VMEM_LIMIT = 48 * 1024 * 1024


def _cparams(sem):
    return pltpu.CompilerParams(dimension_semantics=sem, vmem_limit_bytes=VMEM_LIMIT)


def _sigmoid(z):
    return 1.0 / (1.0 + jnp.exp(-z))


def _silu(z):
    return z * _sigmoid(z)


def _log_sigmoid(z):
    return jnp.minimum(z, 0.0) - jnp.log(1.0 + jnp.exp(-jnp.abs(z)))


def _cumsum_lanes(x):
    n = x.shape[-1]
    lane = lax.broadcasted_iota(jnp.int32, x.shape, x.ndim - 1)
    sh = 1
    while sh < n:
        x = x + jnp.where(lane >= sh, pltpu.roll(x, sh, x.ndim - 1), 0.0)
        sh *= 2
    return x


def _inproj_ab_kernel(x_ref, w_ref, wg_ref, lb_ref, bf_ref,
                      q_ref, lf_ref, k_ref, v_ref, z_ref, fq_ref, fk_ref, fv_ref, fz_ref, c_ref,
                      carry_ref, *, width, fq_scale):
    si = pl.program_id(1)
    xb = x_ref[0].astype(BF16)

    def mm(j):
        return jnp.dot(xb, w_ref[:, j * width:(j + 1) * width], preferred_element_type=F32)

    q_ref[0] = mm(0).astype(BF16)
    lb = lb_ref[...]
    f = lb + (1.0 - lb) * _sigmoid(mm(1))
    lf_ref[0] = jnp.log(f)
    k_ref[0] = (1.0 - f).astype(BF16)
    v_ref[0] = mm(2).astype(BF16)
    z_ref[0] = _silu(mm(3)).astype(BF16)
    fq_ref[0] = (mm(4) * fq_scale).astype(BF16)
    fk_ref[0] = mm(5).astype(BF16)
    fv_ref[0] = mm(6).astype(BF16)
    fz_ref[0] = _silu(mm(7)).astype(BF16)

    g = lax.dot_general(wg_ref[...], xb, (((1,), (1,)), ((), ())), preferred_element_type=F32)
    ls = _log_sigmoid(g + bf_ref[...])

    @pl.when(si == 0)
    def _():
        carry_ref[...] = jnp.zeros_like(carry_ref)

    c = _cumsum_lanes(ls) + carry_ref[:, :1]
    c_ref[0] = c
    carry_ref[...] = jnp.broadcast_to(c[:, -1:], carry_ref.shape)


def _inproj_ab(x, w, wg, lb, bfb, *, tm):
    B, S, D = x.shape
    width = w.shape[1] // 8
    wide = lambda dt: jax.ShapeDtypeStruct((B, S, width), dt)
    tok = pl.BlockSpec((1, tm, width), lambda b, s: (b, s, 0))
    const = lambda shape: pl.BlockSpec(shape, lambda b, s: (0,) * len(shape))
    return pl.pallas_call(
        functools.partial(_inproj_ab_kernel, width=width, fq_scale=HEAD_DK ** -0.5),
        grid=(B, S // tm),
        in_specs=[pl.BlockSpec((1, tm, D), lambda b, s: (b, s, 0)),
                  const(w.shape), const(wg.shape), const(lb.shape), const(bfb.shape)],
        out_specs=[tok, tok, tok, tok, tok, tok, tok, tok, tok,
                   pl.BlockSpec((1, GATE_ROWS, tm), lambda b, s: (b, 0, s))],
        out_shape=[wide(BF16), wide(F32), wide(BF16), wide(BF16), wide(BF16),
                   wide(BF16), wide(BF16), wide(BF16), wide(BF16),
                   jax.ShapeDtypeStruct((B, GATE_ROWS, S), F32)],
        scratch_shapes=[pltpu.VMEM((GATE_ROWS, 128), F32)],
        compiler_params=_cparams(("parallel", "arbitrary")),
        name="inproj_ab",
    )(x, w, wg, lb, bfb)


def _hgrn2_kernel(q_ref, lf_ref, k_ref, v_ref, z_ref, g_ref, o_ref, st_ref, *, T):
    si = pl.program_id(1)

    @pl.when(si == 0)
    def _():
        st_ref[...] = jnp.zeros_like(st_ref)

    L = HG_CHUNK
    nch = T // L
    row = lax.broadcasted_iota(jnp.int32, (T, T), 0)
    col = lax.broadcasted_iota(jnp.int32, (T, T), 1)
    same_chunk_causal = ((row // L) == (col // L)) & (col <= row)
    tri = jnp.where(same_chunk_causal, 1.0, 0.0).astype(BF16)
    last = jnp.where(((row // L) == (col // L)), 1.0, 0.0).astype(BF16)

    for h in range(HEADS):
        sl = slice(h * HEAD_DK, (h + 1) * HEAD_DK)
        lf = lf_ref[0, :, sl]
        lf_hi = lf.astype(BF16)
        lf_lo = (lf - lf_hi.astype(F32)).astype(BF16)
        b = (jnp.dot(tri, lf_hi, preferred_element_type=F32)
             + jnp.dot(tri, lf_lo, preferred_element_type=F32))
        btot = (jnp.dot(last, lf_hi, preferred_element_type=F32)
                + jnp.dot(last, lf_lo, preferred_element_type=F32))
        q = q_ref[0, :, sl].astype(F32)
        k = k_ref[0, :, sl].astype(F32)
        v = v_ref[0, :, sl]
        q_dec = (q * jnp.exp(b)).astype(BF16)
        k_inv = (k * jnp.exp(-b)).astype(BF16)
        k_end = (k * jnp.exp(btot - b)).astype(BF16)
        dec_tot = jnp.exp(btot)

        a = lax.dot_general(q_dec, k_inv, (((1,), (1,)), ((), ())), preferred_element_type=F32)
        a = jnp.where(same_chunk_causal, a, 0.0).astype(BF16)
        o = jnp.dot(a, v, preferred_element_type=F32)

        st = st_ref[h]
        inter = []
        for c in range(nch):
            rs = slice(c * L, (c + 1) * L)
            inter.append(lax.dot_general(q_dec[rs], st.astype(BF16), (((1,), (1,)), ((), ())),
                                         preferred_element_type=F32))
            upd = lax.dot_general(v[rs], k_end[rs], (((0,), (0,)), ((), ())),
                                  preferred_element_type=F32)
            st = st * dec_tot[c * L:c * L + 1, :] + upd
        st_ref[h] = st
        o = o + jnp.concatenate(inter, axis=0)

        ms = jnp.mean(o * o, axis=-1, keepdims=True)
        o = o * lax.rsqrt(ms + RMS_EPS) * g_ref[:, sl]
        o_ref[0, :, sl] = (o * z_ref[0, :, sl].astype(F32)).astype(BF16)


def _hgrn2(q, lf, k, v, z, g, *, T):
    B, S, W = q.shape
    tok = pl.BlockSpec((1, T, W), lambda b, s: (b, s, 0))
    return pl.pallas_call(
        functools.partial(_hgrn2_kernel, T=T),
        grid=(B, S // T),
        in_specs=[tok, tok, tok, tok, tok, pl.BlockSpec((1, W), lambda b, s: (0, 0))],
        out_specs=tok,
        out_shape=jax.ShapeDtypeStruct((B, S, W), BF16),
        scratch_shapes=[pltpu.VMEM((HEADS, HEAD_DK, HEAD_DK), F32)],
        compiler_params=_cparams(("parallel", "arbitrary")),
        name="hgrn2",
    )(q, lf, k, v, z, g)


def _fox_kernel(q_ref, k_ref, v_ref, cq_ref, ck_ref, z_ref, o_ref, m_ref, l_ref, acc_ref, *, tq):
    h = pl.program_id(1)
    qi = pl.program_id(2)
    kj = pl.program_id(3)

    @pl.when(kj == 0)
    def _():
        m_ref[...] = jnp.full_like(m_ref, -jnp.inf)
        l_ref[...] = jnp.zeros_like(l_ref)
        acc_ref[...] = jnp.zeros_like(acc_ref)

    @pl.when(kj <= qi)
    def _():
        s = lax.dot_general(q_ref[0], k_ref[0], (((1,), (1,)), ((), ())), preferred_element_type=F32)
        cref = cq_ref[0, pl.ds(h, 1), :][:, :1]
        s = s + (cref - ck_ref[0, pl.ds(h, 1), :])
        row = lax.broadcasted_iota(jnp.int32, s.shape, 0)
        col = lax.broadcasted_iota(jnp.int32, s.shape, 1)
        s = jnp.where((kj < qi) | (col <= row), s, -jnp.inf)
        m_old = m_ref[...]
        m_new = jnp.maximum(m_old, jnp.max(s, axis=-1, keepdims=True))
        alpha = jnp.exp(m_old - m_new)
        p = jnp.exp(s - m_new)
        l_ref[...] = alpha * l_ref[...] + jnp.sum(p, axis=-1, keepdims=True)
        acc_ref[...] = alpha * acc_ref[...] + jnp.dot(p.astype(BF16), v_ref[0], preferred_element_type=F32)
        m_ref[...] = m_new

    @pl.when(kj == qi)
    def _():
        o = acc_ref[...] / l_ref[...]
        o_ref[0] = (o * z_ref[0].astype(F32)).astype(BF16)


def _fox(q, k, v, c, z, *, tq):
    B, S, W = q.shape
    nq = S // tq
    qspec = pl.BlockSpec((1, tq, HEAD_DK), lambda b, h, i, j: (b, i, h))
    kspec = pl.BlockSpec((1, tq, HEAD_DK), lambda b, h, i, j: (b, jnp.minimum(i, j), h))
    return pl.pallas_call(
        functools.partial(_fox_kernel, tq=tq),
        grid=(B, HEADS, nq, nq),
        in_specs=[qspec, kspec, kspec,
                  pl.BlockSpec((1, GATE_ROWS, tq), lambda b, h, i, j: (b, 0, i)),
                  pl.BlockSpec((1, GATE_ROWS, tq), lambda b, h, i, j: (b, 0, jnp.minimum(i, j))),
                  qspec],
        out_specs=qspec,
        out_shape=jax.ShapeDtypeStruct((B, S, W), BF16),
        scratch_shapes=[pltpu.VMEM((tq, 1), F32), pltpu.VMEM((tq, 1), F32), pltpu.VMEM((tq, HEAD_DK), F32)],
        compiler_params=_cparams(("parallel", "parallel", "parallel", "arbitrary")),
        name="fox_attention",
    )(q, k, v, c, c, z)


def _outproj_ln_kernel(*refs, n_act):
    act_refs = refs[:n_act]
    w_ref, x_ref, g_ref, b_ref, o_ref = refs[n_act:]
    y = ALPHA * x_ref[...]
    off = 0
    for a_ref in act_refs:
        wd = a_ref.shape[1]
        y = y + jnp.dot(a_ref[...], w_ref[off:off + wd, :], preferred_element_type=F32)
        off += wd
    mu = jnp.mean(y, axis=-1, keepdims=True)
    yc = y - mu
    var = jnp.mean(yc * yc, axis=-1, keepdims=True)
    o_ref[...] = yc * lax.rsqrt(var + LN_EPS) * g_ref[...] + b_ref[...]


def _outproj_ln(acts, w, x, g, b, *, tm):
    N, D = x.shape
    tok = lambda wd: pl.BlockSpec((tm, wd), lambda i: (i, 0))
    const = lambda shape: pl.BlockSpec(shape, lambda i: (0,) * len(shape))
    return pl.pallas_call(
        functools.partial(_outproj_ln_kernel, n_act=len(acts)),
        grid=(N // tm,),
        in_specs=[tok(a.shape[1]) for a in acts] + [const(w.shape), tok(D), const(g.shape), const(b.shape)],
        out_specs=tok(D),
        out_shape=jax.ShapeDtypeStruct((N, D), F32),
        compiler_params=_cparams(("parallel",)),
        name="outproj_ln",
    )(*acts, w, x, g, b)


def _inproj_c_kernel(x_ref, w_ref, wg_ref, cw_ref, cb_ref, gb_ref,
                     q_ref, k_ref, v_ref, og_ref, z_ref, gt_ref,
                     ubuf_ref, *, tm, kw, vw, k_scale):
    si = pl.program_id(1)
    xb = x_ref[0].astype(BF16)

    def mm(lo, hi):
        return jnp.dot(xb, w_ref[:, lo:hi], preferred_element_type=F32)

    pad = 8

    @pl.when(si == 0)
    def _():
        ubuf_ref[0:pad, :] = jnp.zeros((pad, 2 * kw), F32)

    ubuf_ref[pad:pad + tm, :] = mm(0, 2 * kw)
    acc = cb_ref[...]
    for j in range(ML_CONV):
        st = pad - (ML_CONV - 1) + j
        acc = acc + cw_ref[j:j + 1, :] * ubuf_ref[st:st + tm, :]
    ubuf_ref[0:pad, :] = ubuf_ref[tm:tm + pad, :]
    qk = _silu(acc)
    q_ref[0] = qk[:, :kw].astype(BF16)
    k_ref[0] = (qk[:, kw:] * k_scale).astype(BF16)
    o0 = 2 * kw
    v_ref[0] = mm(o0, o0 + vw).astype(BF16)
    og_ref[0] = _sigmoid(mm(o0 + vw, o0 + 2 * vw)).astype(BF16)
    z_ref[0] = _silu(mm(o0 + 2 * vw, o0 + 3 * vw)).astype(BF16)

    g = lax.dot_general(wg_ref[...], xb, (((1,), (1,)), ((), ())), preferred_element_type=F32) + gb_ref[...]
    rowi = lax.broadcasted_iota(jnp.int32, g.shape, 0)
    gt_ref[0] = jnp.where(rowi < HEADS, g, _log_sigmoid(g))


def _inproj_c(x, w, wg, cw, cb, gb, *, tm):
    B, S, D = x.shape
    kw = cw.shape[1] // 2
    vw = (w.shape[1] - 2 * kw) // 3
    const = lambda shape: pl.BlockSpec(shape, lambda b, s: (0,) * len(shape))
    tok = lambda wd: pl.BlockSpec((1, tm, wd), lambda b, s: (b, s, 0))
    return pl.pallas_call(
        functools.partial(_inproj_c_kernel, tm=tm, kw=kw, vw=vw, k_scale=HEAD_DK ** -0.5),
        grid=(B, S // tm),
        in_specs=[pl.BlockSpec((1, tm, D), lambda b, s: (b, s, 0)),
                  const(w.shape), const(wg.shape), const(cw.shape), const(cb.shape), const(gb.shape)],
        out_specs=[tok(kw), tok(kw), tok(vw), tok(vw), tok(vw),
                   pl.BlockSpec((1, GATE_ROWS, tm), lambda b, s: (b, 0, s))],
        out_shape=[jax.ShapeDtypeStruct((B, S, kw), BF16), jax.ShapeDtypeStruct((B, S, kw), BF16),
                   jax.ShapeDtypeStruct((B, S, vw), BF16), jax.ShapeDtypeStruct((B, S, vw), BF16),
                   jax.ShapeDtypeStruct((B, S, vw), BF16), jax.ShapeDtypeStruct((B, GATE_ROWS, S), F32)],
        scratch_shapes=[pltpu.VMEM((tm + 8, 2 * kw), F32)],
        compiler_params=_cparams(("parallel", "arbitrary")),
        name="inproj_c",
    )(x, w, wg, cw, cb, gb)


def _mlstm_kernel(q_ref, k_ref, v_ref, gt_ref, og_ref, z_ref, g_ref, o_ref, c_ref, n_ref, m_ref, *, L):
    h = pl.program_id(1)
    ci = pl.program_id(2)

    @pl.when(ci == 0)
    def _():
        c_ref[...] = jnp.zeros_like(c_ref)
        n_ref[...] = jnp.zeros_like(n_ref)
        m_ref[...] = jnp.zeros_like(m_ref)

    q = q_ref[0]
    k = k_ref[0]
    v = v_ref[0]
    i_row = gt_ref[0, pl.ds(h, 1), :]
    g_row = gt_ref[0, pl.ds(h + HEADS, 1), :]
    b_row = _cumsum_lanes(g_row)
    a_row = i_row - b_row
    m_prev = m_ref[:, :1]

    row = lax.broadcasted_iota(jnp.int32, (L, L), 0)
    col = lax.broadcasted_iota(jnp.int32, (L, L), 1)
    tril = col <= row
    M_col = jnp.maximum(m_prev, jnp.max(jnp.where(tril, a_row, -jnp.inf), axis=-1, keepdims=True))
    b_col = jnp.sum(jnp.where(tril, g_row, 0.0), axis=-1, keepdims=True)
    w = jnp.where(tril, jnp.exp(a_row - M_col), 0.0)
    s_inter = jnp.exp(m_prev - M_col)

    qk = lax.dot_general(q, k, (((1,), (1,)), ((), ())), preferred_element_type=F32) * w
    num = (jnp.dot(qk.astype(BF16), v, preferred_element_type=F32)
           + s_inter * jnp.dot(q, c_ref[...].astype(BF16), preferred_element_type=F32))
    den = (jnp.sum(qk, axis=-1, keepdims=True)
           + s_inter * jnp.sum(q.astype(F32) * n_ref[...], axis=-1, keepdims=True))
    hval = num / jnp.maximum(jnp.abs(den), jnp.exp(-(b_col + M_col)))

    M_last = jnp.maximum(m_prev, jnp.max(a_row, axis=-1, keepdims=True))
    wk_row = jnp.exp(a_row - M_last)
    decay = jnp.exp(m_prev - M_last)
    wk_col = jnp.sum(jnp.where(row == col, wk_row, 0.0), axis=-1, keepdims=True)
    kw = (k.astype(F32) * wk_col).astype(BF16)
    c_ref[...] = decay * c_ref[...] + lax.dot_general(kw, v, (((0,), (0,)), ((), ())),
                                                      preferred_element_type=F32)
    wk8 = jnp.broadcast_to(wk_row, (8, L)).astype(BF16)
    n_ref[...] = decay * n_ref[...] + jnp.dot(wk8, k, preferred_element_type=F32)[:1]
    m_ref[...] = jnp.broadcast_to(b_row[:, L - 1:] + M_last, m_ref.shape)

    ht = og_ref[0].astype(F32) * hval
    ms = jnp.mean(ht * ht, axis=-1, keepdims=True)
    ht = ht * lax.rsqrt(ms + RMS_EPS) * g_ref[...]
    o_ref[0] = (ht * z_ref[0].astype(F32)).astype(BF16)


def _mlstm(q, k, v, gt, og, z, g, *, L):
    B, S, _ = q.shape
    W = v.shape[2]
    kspec = pl.BlockSpec((1, L, HEAD_DK), lambda b, h, c: (b, c, h))
    vspec = pl.BlockSpec((1, L, ML_DV), lambda b, h, c: (b, c, h))
    return pl.pallas_call(
        functools.partial(_mlstm_kernel, L=L),
        grid=(B, HEADS, S // L),
        in_specs=[kspec, kspec, vspec,
                  pl.BlockSpec((1, GATE_ROWS, L), lambda b, h, c: (b, 0, c)),
                  vspec, vspec, pl.BlockSpec((1, ML_DV), lambda b, h, c: (0, h))],
        out_specs=vspec,
        out_shape=jax.ShapeDtypeStruct((B, S, W), BF16),
        scratch_shapes=[pltpu.VMEM((HEAD_DK, ML_DV), F32), pltpu.VMEM((1, HEAD_DK), F32),
                        pltpu.VMEM((1, 128), F32)],
        compiler_params=_cparams(("parallel", "parallel", "arbitrary")),
        name="mlstm",
    )(q, k, v, gt, og, z, g)


def _gate_rows(w_cols):
    wt = w_cols.T
    return jnp.pad(wt, ((0, GATE_ROWS - wt.shape[0]), (0, 0))).astype(BF16)


def kernel(x, hgrn_lb_logits, ab_w_in, ab_fox_bf, ab_hgrn_norm_g, ab_w_out, c_w_in,
           c_conv_w, c_conv_b, c_bi, c_bf, c_norm_g, c_w_out, ln_g, ln_b):
    B, S, D = x.shape
    W = HEADS * HEAD_DK
    TM = min(512, S)

    lb_table = jnp.cumsum(jax.nn.softmax(hgrn_lb_logits.astype(F32), axis=0), axis=0)
    h = x.astype(F32)

    w_in = ab_w_in[0].astype(F32)
    w_main = jnp.concatenate([w_in[:, :7 * W], w_in[:, 7 * W + HEADS:]], axis=1).astype(BF16)
    w_gate = _gate_rows(w_in[:, 7 * W:7 * W + HEADS])
    bfb = jnp.broadcast_to(jnp.pad(ab_fox_bf[0].astype(F32), (0, GATE_ROWS - HEADS))[:, None], (GATE_ROWS, TM))
    hq, lf, hk, hv, hz, fq, fk, fv, fz, c = _inproj_ab(
        h, w_main, w_gate, lb_table[0][None, :], bfb, tm=TM)
    hg = _hgrn2(hq, lf, hk, hv, hz, ab_hgrn_norm_g[0].astype(F32)[None, :], T=min(256, S))
    fx = _fox(fq, fk, fv, c, fz, tq=min(512, S))
    h = _outproj_ln([hg.reshape(B * S, W), fx.reshape(B * S, W)], ab_w_out[0].astype(BF16),
                    h.reshape(B * S, D), ln_g[0].astype(F32)[None, :], ln_b[0].astype(F32)[None, :],
                    tm=TM).reshape(B, S, D)

    w_in = c_w_in[0].astype(F32)
    KW, VW = W, HEADS * ML_DV
    o_i = 2 * KW + VW
    w_main = jnp.concatenate([w_in[:, :o_i], w_in[:, o_i + 2 * HEADS:]], axis=1).astype(BF16)
    w_gate = _gate_rows(w_in[:, o_i:o_i + 2 * HEADS])
    gb = jnp.broadcast_to(jnp.concatenate([c_bi[0], c_bf[0]]).astype(F32)[:, None], (GATE_ROWS, TM))
    mq, mk, mv, og, mz, gt = _inproj_c(h, w_main, w_gate, c_conv_w[0].astype(F32),
                                       c_conv_b[0].astype(F32)[None, :], gb, tm=TM)
    ht = _mlstm(mq, mk, mv, gt, og, mz, c_norm_g[0].astype(F32)[None, :], L=min(256, S))
    h = _outproj_ln([ht.reshape(B * S, VW)], c_w_out[0].astype(BF16), h.reshape(B * S, D),
                    ln_g[1].astype(F32)[None, :], ln_b[1].astype(F32)[None, :], tm=TM).reshape(B, S, D)
    return h.astype(x.dtype)
```

```python
import functools
import math

import jax
import jax.numpy as jnp
from jax import lax
from jax.experimental import pallas as pl
from jax.experimental.pallas import tpu as pltpu

F32 = jnp.float32
BF16 = jnp.bfloat16

DEPTH = 2
ALPHA = (2 * DEPTH) ** 0.25
LN_EPS = 1e-5
RMS_EPS = 1e-6

HEADS = 4
HEAD_DK = 128
ML_DV = 256
HG_CHUNK = 32
ML_CONV = 4
GATE_ROWS = 8

VMEM_LIMIT = 48 * 1024 * 1024
LOG2E = math.log2(math.e)


def _cparams(sem):
    return pltpu.CompilerParams(dimension_semantics=sem, vmem_limit_bytes=VMEM_LIMIT)


def _sigmoid(z):
    return 1.0 / (1.0 + jnp.exp(-z))


def _silu(z):
    return z * _sigmoid(z)


def _log_sigmoid(z):
    return jnp.minimum(z, 0.0) - jnp.log(1.0 + jnp.exp(-jnp.abs(z)))


def _cumsum_lanes(x):
    n = x.shape[-1]
    lane = lax.broadcasted_iota(jnp.int32, x.shape, x.ndim - 1)
    sh = 1
    while sh < n:
        x = x + jnp.where(lane >= sh, pltpu.roll(x, sh, x.ndim - 1), 0.0)
        sh *= 2
    return x


def _inproj_ab_kernel(x_ref, w_ref, wg_ref, lb_ref, bf_ref,
                      q_ref, lf_ref, k_ref, v_ref, z_ref, fq_ref, fk_ref, fv_ref, fz_ref, c_ref,
                      carry_ref, *, width, fq_scale):
    si = pl.program_id(1)
    xb = x_ref[0].astype(BF16)

    def mm(j):
        return jnp.dot(xb, w_ref[:, j * width:(j + 1) * width], preferred_element_type=F32)

    q_ref[0] = mm(0).astype(BF16)
    lb = lb_ref[...]
    f = lb + (1.0 - lb) * _sigmoid(mm(1))
    lf_ref[0] = jnp.log(f)
    k_ref[0] = (1.0 - f).astype(BF16)
    v_ref[0] = mm(2).astype(BF16)
    z_ref[0] = _silu(mm(3)).astype(BF16)
    fq_ref[0] = (mm(4) * fq_scale).astype(BF16)
    fk_ref[0] = mm(5).astype(BF16)
    fv_ref[0] = mm(6).astype(BF16)
    fz_ref[0] = _silu(mm(7)).astype(BF16)

    g = lax.dot_general(wg_ref[...], xb, (((1,), (1,)), ((), ())), preferred_element_type=F32)
    ls = _log_sigmoid(g + bf_ref[...])

    @pl.when(si == 0)
    def _():
        carry_ref[...] = jnp.zeros_like(carry_ref)

    c = _cumsum_lanes(ls) + carry_ref[:, :1]
    c_ref[0] = c
    carry_ref[...] = jnp.broadcast_to(c[:, -1:], carry_ref.shape)


def _inproj_ab(x, w, wg, lb, bfb, *, tm):
    B, S, D = x.shape
    width = w.shape[1] // 8
    wide = lambda dt: jax.ShapeDtypeStruct((B, S, width), dt)
    tok = pl.BlockSpec((1, tm, width), lambda b, s: (b, s, 0))
    const = lambda shape: pl.BlockSpec(shape, lambda b, s: (0,) * len(shape))
    return pl.pallas_call(
        functools.partial(_inproj_ab_kernel, width=width, fq_scale=HEAD_DK ** -0.5 * LOG2E),
        grid=(B, S // tm),
        in_specs=[pl.BlockSpec((1, tm, D), lambda b, s: (b, s, 0)),
                  const(w.shape), const(wg.shape), const(lb.shape), const(bfb.shape)],
        out_specs=[tok, tok, tok, tok, tok, tok, tok, tok, tok,
                   pl.BlockSpec((1, GATE_ROWS, tm), lambda b, s: (b, 0, s))],
        out_shape=[wide(BF16), wide(F32), wide(BF16), wide(BF16), wide(BF16),
                   wide(BF16), wide(BF16), wide(BF16), wide(BF16),
                   jax.ShapeDtypeStruct((B, GATE_ROWS, S), F32)],
        scratch_shapes=[pltpu.VMEM((GATE_ROWS, 128), F32)],
        compiler_params=_cparams(("parallel", "arbitrary")),
        name="inproj_ab",
    )(x, w, wg, lb, bfb)


def _hgrn2_kernel(q_ref, lf_ref, k_ref, v_ref, z_ref, g_ref, o_ref, st_ref, *, T):
    si = pl.program_id(1)

    @pl.when(si == 0)
    def _():
        st_ref[...] = jnp.zeros_like(st_ref)

    L = HG_CHUNK
    nch = T // L
    row = lax.broadcasted_iota(jnp.int32, (T, T), 0)
    col = lax.broadcasted_iota(jnp.int32, (T, T), 1)
    same_chunk_causal = ((row // L) == (col // L)) & (col <= row)
    tri = jnp.where(same_chunk_causal, 1.0, 0.0).astype(BF16)
    last = jnp.where(((row // L) == (col // L)), 1.0, 0.0).astype(BF16)

    for h in range(HEADS):
        sl = slice(h * HEAD_DK, (h + 1) * HEAD_DK)
        lf = lf_ref[0, :, sl]
        lf_hi = lf.astype(BF16)
        lf_lo = (lf - lf_hi.astype(F32)).astype(BF16)
        b = (jnp.dot(tri, lf_hi, preferred_element_type=F32)
             + jnp.dot(tri, lf_lo, preferred_element_type=F32))
        btot = (jnp.dot(last, lf_hi, preferred_element_type=F32)
                + jnp.dot(last, lf_lo, preferred_element_type=F32))
        q = q_ref[0, :, sl].astype(F32)
        k = k_ref[0, :, sl].astype(F32)
        v = v_ref[0, :, sl]
        q_dec = (q * jnp.exp(b)).astype(BF16)
        k_inv = (k * jnp.exp(-b)).astype(BF16)
        k_end = (k * jnp.exp(btot - b)).astype(BF16)
        dec_tot = jnp.exp(btot)

        a = lax.dot_general(q_dec, k_inv, (((1,), (1,)), ((), ())), preferred_element_type=F32)
        a = jnp.where(same_chunk_causal, a, 0.0).astype(BF16)
        o = jnp.dot(a, v, preferred_element_type=F32)

        st = st_ref[h]
        inter = []
        for c in range(nch):
            rs = slice(c * L, (c + 1) * L)
            inter.append(lax.dot_general(q_dec[rs], st.astype(BF16), (((1,), (1,)), ((), ())),
                                         preferred_element_type=F32))
            upd = lax.dot_general(v[rs], k_end[rs], (((0,), (0,)), ((), ())),
                                  preferred_element_type=F32)
            st = st * dec_tot[c * L:c * L + 1, :] + upd
        st_ref[h] = st
        o = o + jnp.concatenate(inter, axis=0)

        ms = jnp.mean(o * o, axis=-1, keepdims=True)
        o = o * lax.rsqrt(ms + RMS_EPS) * g_ref[:, sl]
        o_ref[0, :, sl] = (o * z_ref[0, :, sl].astype(F32)).astype(BF16)


def _hgrn2(q, lf, k, v, z, g, *, T):
    B, S, W = q.shape
    tok = pl.BlockSpec((1, T, W), lambda b, s: (b, s, 0))
    return pl.pallas_call(
        functools.partial(_hgrn2_kernel, T=T),
        grid=(B, S // T),
        in_specs=[tok, tok, tok, tok, tok, pl.BlockSpec((1, W), lambda b, s: (0, 0))],
        out_specs=tok,
        out_shape=jax.ShapeDtypeStruct((B, S, W), BF16),
        scratch_shapes=[pltpu.VMEM((HEADS, HEAD_DK, HEAD_DK), F32)],
        compiler_params=_cparams(("parallel", "arbitrary")),
        name="hgrn2",
    )(q, lf, k, v, z, g)


def _fox_kernel(q_ref, k_ref, v_ref, c_ref, z_ref, o_ref, vaug_ref, m_ref, acc_ref, *, tq, tk):
    h = pl.program_id(1)
    qi = pl.program_id(2)

    @pl.when(qi == 0)
    def _():
        vaug_ref[:, :HEAD_DK] = v_ref[0]
        vaug_ref[:, HEAD_DK:] = jnp.ones((vaug_ref.shape[0], HEAD_DK), BF16)

    per_q = tq // tk

    def c_row(j):
        return c_ref[0, j, pl.ds(h, 1), :] * LOG2E

    cref = c_row(qi * per_q)[:, :1]
    m_ref[...] = jnp.full_like(m_ref, -jnp.inf)
    acc_ref[...] = jnp.zeros_like(acc_ref)

    def chunk(j, r0, masked):
        ks = pl.multiple_of(j * tk, tk)
        s = lax.dot_general(q_ref[0, r0:, :], k_ref[0, pl.ds(ks, tk), :], (((1,), (1,)), ((), ())),
                            preferred_element_type=F32)
        s = s + (cref - c_row(j))
        if masked:
            row = lax.broadcasted_iota(jnp.int32, s.shape, 0)
            col = lax.broadcasted_iota(jnp.int32, s.shape, 1)
            s = jnp.where(col <= row, s, -jnp.inf)
        m_old = m_ref[r0:, :]
        m_new = jnp.maximum(m_old, jnp.max(s, axis=-1, keepdims=True))
        alpha = jnp.exp2(m_old - m_new)
        p = jnp.exp2(s - jnp.tile(m_new, (1, tk // 128))).astype(BF16)
        acc_ref[r0:, :] = (jnp.tile(alpha, (1, 2)) * acc_ref[r0:, :]
                           + jnp.dot(p, vaug_ref[pl.ds(ks, tk), :], preferred_element_type=F32))
        m_ref[r0:, :] = m_new

    def body(i, carry):
        for u in range(per_q):
            chunk(i * per_q + u, 0, False)
        return carry

    lax.fori_loop(0, qi, body, 0)
    for u in range(per_q):
        chunk(qi * per_q + u, u * tk, True)
    acc = acc_ref[...]
    o = acc[:, :HEAD_DK] / acc[:, HEAD_DK:]
    o_ref[0] = (o * z_ref[0].astype(F32)).astype(BF16)


def _fox(q, k, v, c, z, *, tq, tk):
    B, S, W = q.shape
    nq, nk = S // tq, S // tk
    c4 = c.reshape(B, GATE_ROWS, nk, tk).transpose(0, 2, 1, 3)
    qspec = pl.BlockSpec((1, tq, HEAD_DK), lambda b, h, i: (b, i, h))
    kspec = pl.BlockSpec((1, S, HEAD_DK), lambda b, h, i: (b, 0, h))
    return pl.pallas_call(
        functools.partial(_fox_kernel, tq=tq, tk=tk),
        grid=(B, HEADS, nq),
        in_specs=[qspec, kspec, kspec,
                  pl.BlockSpec((1, nk, GATE_ROWS, tk), lambda b, h, i: (b, 0, 0, 0)),
                  qspec],
        out_specs=qspec,
        out_shape=jax.ShapeDtypeStruct((B, S, W), BF16),
        scratch_shapes=[pltpu.VMEM((S, 2 * HEAD_DK), BF16), pltpu.VMEM((tq, 128), F32),
                        pltpu.VMEM((tq, 2 * HEAD_DK), F32)],
        compiler_params=_cparams(("parallel", "parallel", "arbitrary")),
        name="fox_attention",
    )(q, k, v, c4, z)


def _outproj_ln_kernel(*refs, n_act):
    act_refs = refs[:n_act]
    w_ref, x_ref, g_ref, b_ref, o_ref = refs[n_act:]
    y = ALPHA * x_ref[...]
    off = 0
    for a_ref in act_refs:
        wd = a_ref.shape[1]
        y = y + jnp.dot(a_ref[...], w_ref[off:off + wd, :], preferred_element_type=F32)
        off += wd
    mu = jnp.mean(y, axis=-1, keepdims=True)
    yc = y - mu
    var = jnp.mean(yc * yc, axis=-1, keepdims=True)
    o_ref[...] = yc * lax.rsqrt(var + LN_EPS) * g_ref[...] + b_ref[...]


def _outproj_ln(acts, w, x, g, b, *, tm):
    N, D = x.shape
    tok = lambda wd: pl.BlockSpec((tm, wd), lambda i: (i, 0))
    const = lambda shape: pl.BlockSpec(shape, lambda i: (0,) * len(shape))
    return pl.pallas_call(
        functools.partial(_outproj_ln_kernel, n_act=len(acts)),
        grid=(N // tm,),
        in_specs=[tok(a.shape[1]) for a in acts] + [const(w.shape), tok(D), const(g.shape), const(b.shape)],
        out_specs=tok(D),
        out_shape=jax.ShapeDtypeStruct((N, D), F32),
        compiler_params=_cparams(("parallel",)),
        name="outproj_ln",
    )(*acts, w, x, g, b)


def _inproj_c_kernel(x_ref, w_ref, wg_ref, cw_ref, cb_ref, gb_ref,
                     q_ref, k_ref, v_ref, og_ref, z_ref, gt_ref,
                     ubuf_ref, *, tm, kw, vw, k_scale):
    si = pl.program_id(1)
    xb = x_ref[0].astype(BF16)

    def mm(lo, hi):
        return jnp.dot(xb, w_ref[:, lo:hi], preferred_element_type=F32)

    pad = 8

    @pl.when(si == 0)
    def _():
        ubuf_ref[0:pad, :] = jnp.zeros((pad, 2 * kw), F32)

    ubuf_ref[pad:pad + tm, :] = mm(0, 2 * kw)
    acc = cb_ref[...]
    for j in range(ML_CONV):
        st = pad - (ML_CONV - 1) + j
        acc = acc + cw_ref[j:j + 1, :] * ubuf_ref[st:st + tm, :]
    ubuf_ref[0:pad, :] = ubuf_ref[tm:tm + pad, :]
    qk = _silu(acc)
    q_ref[0] = qk[:, :kw].astype(BF16)
    k_ref[0] = (qk[:, kw:] * k_scale).astype(BF16)
    o0 = 2 * kw
    v_ref[0] = mm(o0, o0 + vw).astype(BF16)
    og_ref[0] = _sigmoid(mm(o0 + vw, o0 + 2 * vw)).astype(BF16)
    z_ref[0] = _silu(mm(o0 + 2 * vw, o0 + 3 * vw)).astype(BF16)

    g = lax.dot_general(wg_ref[...], xb, (((1,), (1,)), ((), ())), preferred_element_type=F32) + gb_ref[...]
    rowi = lax.broadcasted_iota(jnp.int32, g.shape, 0)
    gt_ref[0] = jnp.where(rowi < HEADS, g, _log_sigmoid(g))


def _inproj_c(x, w, wg, cw, cb, gb, *, tm):
    B, S, D = x.shape
    kw = cw.shape[1] // 2
    vw = (w.shape[1] - 2 * kw) // 3
    const = lambda shape: pl.BlockSpec(shape, lambda b, s: (0,) * len(shape))
    tok = lambda wd: pl.BlockSpec((1, tm, wd), lambda b, s: (b, s, 0))
    return pl.pallas_call(
        functools.partial(_inproj_c_kernel, tm=tm, kw=kw, vw=vw, k_scale=HEAD_DK ** -0.5),
        grid=(B, S // tm),
        in_specs=[pl.BlockSpec((1, tm, D), lambda b, s: (b, s, 0)),
                  const(w.shape), const(wg.shape), const(cw.shape), const(cb.shape), const(gb.shape)],
        out_specs=[tok(kw), tok(kw), tok(vw), tok(vw), tok(vw),
                   pl.BlockSpec((1, GATE_ROWS, tm), lambda b, s: (b, 0, s))],
        out_shape=[jax.ShapeDtypeStruct((B, S, kw), BF16), jax.ShapeDtypeStruct((B, S, kw), BF16),
                   jax.ShapeDtypeStruct((B, S, vw), BF16), jax.ShapeDtypeStruct((B, S, vw), BF16),
                   jax.ShapeDtypeStruct((B, S, vw), BF16), jax.ShapeDtypeStruct((B, GATE_ROWS, S), F32)],
        scratch_shapes=[pltpu.VMEM((tm + 8, 2 * kw), F32)],
        compiler_params=_cparams(("parallel", "arbitrary")),
        name="inproj_c",
    )(x, w, wg, cw, cb, gb)


def _mlstm_kernel(q_ref, k_ref, v_ref, gt_ref, og_ref, z_ref, g_ref, o_ref, c_ref, n_ref, m_ref, *, L):
    h = pl.program_id(1)
    ci = pl.program_id(2)

    @pl.when(ci == 0)
    def _():
        c_ref[...] = jnp.zeros_like(c_ref)
        n_ref[...] = jnp.zeros_like(n_ref)
        m_ref[...] = jnp.zeros_like(m_ref)

    q = q_ref[0]
    k = k_ref[0]
    v = v_ref[0]
    i_row = gt_ref[0, pl.ds(h, 1), :]
    g_row = gt_ref[0, pl.ds(h + HEADS, 1), :]
    b_row = _cumsum_lanes(g_row)
    a_row = i_row - b_row
    m_prev = m_ref[:, :1]

    row = lax.broadcasted_iota(jnp.int32, (L, L), 0)
    col = lax.broadcasted_iota(jnp.int32, (L, L), 1)
    tril = col <= row
    M_col = jnp.maximum(m_prev, jnp.max(jnp.where(tril, a_row, -jnp.inf), axis=-1, keepdims=True))
    b_col = jnp.sum(jnp.where(tril, g_row, 0.0), axis=-1, keepdims=True)
    w = jnp.where(tril, jnp.exp(a_row - M_col), 0.0)
    s_inter = jnp.exp(m_prev - M_col)

    qk = lax.dot_general(q, k, (((1,), (1,)), ((), ())), preferred_element_type=F32) * w
    num = (jnp.dot(qk.astype(BF16), v, preferred_element_type=F32)
           + s_inter * jnp.dot(q, c_ref[...].astype(BF16), preferred_element_type=F32))
    den = (jnp.sum(qk, axis=-1, keepdims=True)
           + s_inter * jnp.sum(q.astype(F32) * n_ref[...], axis=-1, keepdims=True))
    hval = num / jnp.maximum(jnp.abs(den), jnp.exp(-(b_col + M_col)))

    M_last = jnp.maximum(m_prev, jnp.max(a_row, axis=-1, keepdims=True))
    wk_row = jnp.exp(a_row - M_last)
    decay = jnp.exp(m_prev - M_last)
    wk_col = jnp.sum(jnp.where(row == col, wk_row, 0.0), axis=-1, keepdims=True)
    kw = (k.astype(F32) * wk_col).astype(BF16)
    c_ref[...] = decay * c_ref[...] + lax.dot_general(kw, v, (((0,), (0,)), ((), ())),
                                                      preferred_element_type=F32)
    wk8 = jnp.broadcast_to(wk_row, (8, L)).astype(BF16)
    n_ref[...] = decay * n_ref[...] + jnp.dot(wk8, k, preferred_element_type=F32)[:1]
    m_ref[...] = jnp.broadcast_to(b_row[:, L - 1:] + M_last, m_ref.shape)

    ht = og_ref[0].astype(F32) * hval
    ms = jnp.mean(ht * ht, axis=-1, keepdims=True)
    ht = ht * lax.rsqrt(ms + RMS_EPS) * g_ref[...]
    o_ref[0] = (ht * z_ref[0].astype(F32)).astype(BF16)


def _mlstm(q, k, v, gt, og, z, g, *, L):
    B, S, _ = q.shape
    W = v.shape[2]
    kspec = pl.BlockSpec((1, L, HEAD_DK), lambda b, h, c: (b, c, h))
    vspec = pl.BlockSpec((1, L, ML_DV), lambda b, h, c: (b, c, h))
    return pl.pallas_call(
        functools.partial(_mlstm_kernel, L=L),
        grid=(B, HEADS, S // L),
        in_specs=[kspec, kspec, vspec,
                  pl.BlockSpec((1, GATE_ROWS, L), lambda b, h, c: (b, 0, c)),
                  vspec, vspec, pl.BlockSpec((1, ML_DV), lambda b, h, c: (0, h))],
        out_specs=vspec,
        out_shape=jax.ShapeDtypeStruct((B, S, W), BF16),
        scratch_shapes=[pltpu.VMEM((HEAD_DK, ML_DV), F32), pltpu.VMEM((1, HEAD_DK), F32),
                        pltpu.VMEM((1, 128), F32)],
        compiler_params=_cparams(("parallel", "parallel", "arbitrary")),
        name="mlstm",
    )(q, k, v, gt, og, z, g)


def _gate_rows(w_cols):
    wt = w_cols.T
    return jnp.pad(wt, ((0, GATE_ROWS - wt.shape[0]), (0, 0))).astype(BF16)


def kernel(x, hgrn_lb_logits, ab_w_in, ab_fox_bf, ab_hgrn_norm_g, ab_w_out, c_w_in,
           c_conv_w, c_conv_b, c_bi, c_bf, c_norm_g, c_w_out, ln_g, ln_b):
    B, S, D = x.shape
    W = HEADS * HEAD_DK
    TM = min(512, S)

    lb_table = jnp.cumsum(jax.nn.softmax(hgrn_lb_logits.astype(F32), axis=0), axis=0)
    h = x.astype(F32)

    w_in = ab_w_in[0].astype(F32)
    w_main = jnp.concatenate([w_in[:, :7 * W], w_in[:, 7 * W + HEADS:]], axis=1).astype(BF16)
    w_gate = _gate_rows(w_in[:, 7 * W:7 * W + HEADS])
    bfb = jnp.broadcast_to(jnp.pad(ab_fox_bf[0].astype(F32), (0, GATE_ROWS - HEADS))[:, None], (GATE_ROWS, TM))
    hq, lf, hk, hv, hz, fq, fk, fv, fz, c = _inproj_ab(
        h, w_main, w_gate, lb_table[0][None, :], bfb, tm=TM)
    hg = _hgrn2(hq, lf, hk, hv, hz, ab_hgrn_norm_g[0].astype(F32)[None, :], T=min(256, S))
    fx = _fox(fq, fk, fv, c, fz, tq=min(1024, S), tk=min(512, S))
    h = _outproj_ln([hg.reshape(B * S, W), fx.reshape(B * S, W)], ab_w_out[0].astype(BF16),
                    h.reshape(B * S, D), ln_g[0].astype(F32)[None, :], ln_b[0].astype(F32)[None, :],
                    tm=TM).reshape(B, S, D)

    w_in = c_w_in[0].astype(F32)
    KW, VW = W, HEADS * ML_DV
    o_i = 2 * KW + VW
    w_main = jnp.concatenate([w_in[:, :o_i], w_in[:, o_i + 2 * HEADS:]], axis=1).astype(BF16)
    w_gate = _gate_rows(w_in[:, o_i:o_i + 2 * HEADS])
    gb = jnp.broadcast_to(jnp.concatenate([c_bi[0], c_bf[0]]).astype(F32)[:, None], (GATE_ROWS, TM))
    mq, mk, mv, og, mz, gt = _inproj_c(h, w_main, w_gate, c_conv_w[0].astype(F32),
                                       c_conv_b[0].astype(F32)[None, :], gb, tm=TM)
    ht = _mlstm(mq, mk, mv, gt, og, mz, c_norm_g[0].astype(F32)[None, :], L=min(256, S))
    h = _outproj_ln([ht.reshape(B * S, VW)], c_w_out[0].astype(BF16), h.reshape(B * S, D),
                    ln_g[1].astype(F32)[None, :], ln_b[1].astype(F32)[None, :], tm=TM).reshape(B, S, D)
    return h.astype(x.dtype)
```

```python
import functools
import math

import jax
import jax.numpy as jnp
from jax import lax
from jax.experimental import pallas as pl
from jax.experimental.pallas import tpu as pltpu

F32 = jnp.float32
BF16 = jnp.bfloat16

DEPTH = 2
ALPHA = (2 * DEPTH) ** 0.25
LN_EPS = 1e-5
RMS_EPS = 1e-6

HEADS = 4
HEAD_DK = 128
ML_DV = 256
HG_CHUNK = 32
ML_CONV = 4
ML_CHUNK = 256
GATE_ROWS = 8

VMEM_LIMIT = 56 * 1024 * 1024
LOG2E = math.log2(math.e)


def _cparams(sem):
    return pltpu.CompilerParams(dimension_semantics=sem, vmem_limit_bytes=VMEM_LIMIT)


def _sigmoid(z):
    return 0.5 + 0.5 * jnp.tanh(0.5 * z)


def _silu(z):
    hz = 0.5 * z
    return hz + hz * jnp.tanh(hz)


def _log_sigmoid(z):
    return jnp.minimum(z, 0.0) - jnp.log(1.0 + jnp.exp(-jnp.abs(z)))


def _cumsum_lanes(x, seg=None):
    seg = seg or x.shape[-1]
    pos = lax.broadcasted_iota(jnp.int32, x.shape, x.ndim - 1) % seg
    sh = 1
    while sh < seg:
        x = x + jnp.where(pos >= sh, pltpu.roll(x, sh, x.ndim - 1), 0.0)
        sh *= 2
    return x


def _inproj_ab_kernel(x_ref, w_ref, wg_ref, lb_ref, bf_ref,
                      q_ref, lf_ref, k_ref, v_ref, z_ref, fq_ref, fk_ref, fv_ref, fz_ref, c_ref,
                      carry_ref, *, width, fq_scale):
    si = pl.program_id(1)
    xb = x_ref[0].astype(BF16)

    g = lax.dot_general(wg_ref[...], xb, (((1,), (1,)), ((), ())), preferred_element_type=F32)
    ls = _log_sigmoid(g + bf_ref[...])

    @pl.when(si == 0)
    def _():
        carry_ref[...] = jnp.zeros_like(carry_ref)

    c = _cumsum_lanes(ls) + carry_ref[:, :1]
    c_ref[0] = c
    carry_ref[...] = jnp.broadcast_to(c[:, -1:], carry_ref.shape)

    def mm(j):
        return jnp.dot(xb, w_ref[:, j * width:(j + 1) * width], preferred_element_type=F32)

    q_ref[0] = mm(0).astype(BF16)
    lb = lb_ref[...]
    f = lb + (1.0 - lb) * _sigmoid(mm(1))
    lf_ref[0] = jnp.log(f)
    k_ref[0] = (1.0 - f).astype(BF16)
    v_ref[0] = mm(2).astype(BF16)
    z_ref[0] = _silu(mm(3)).astype(BF16)
    fq_ref[0] = (mm(4) * fq_scale).astype(BF16)
    fk_ref[0] = mm(5).astype(BF16)
    fv_ref[0] = mm(6).astype(BF16)
    fz_ref[0] = _silu(mm(7)).astype(BF16)


def _inproj_ab(x, w, wg, lb, bfb, *, tm):
    B, S, D = x.shape
    width = w.shape[1] // 8
    wide = lambda dt: jax.ShapeDtypeStruct((B, S, width), dt)
    tok = pl.BlockSpec((1, tm, width), lambda b, s: (b, s, 0))
    const = lambda shape: pl.BlockSpec(shape, lambda b, s: (0,) * len(shape))
    return pl.pallas_call(
        functools.partial(_inproj_ab_kernel, width=width, fq_scale=HEAD_DK ** -0.5 * LOG2E),
        grid=(B, S // tm),
        in_specs=[pl.BlockSpec((1, tm, D), lambda b, s: (b, s, 0)),
                  const(w.shape), const(wg.shape), const(lb.shape), const(bfb.shape)],
        out_specs=[tok, tok, tok, tok, tok, tok, tok, tok, tok,
                   pl.BlockSpec((1, GATE_ROWS, tm), lambda b, s: (b, 0, s))],
        out_shape=[wide(BF16), wide(F32), wide(BF16), wide(BF16), wide(BF16),
                   wide(BF16), wide(BF16), wide(BF16), wide(BF16),
                   jax.ShapeDtypeStruct((B, GATE_ROWS, S), F32)],
        scratch_shapes=[pltpu.VMEM((GATE_ROWS, 128), F32)],
        compiler_params=_cparams(("parallel", "arbitrary")),
        name="inproj_ab",
    )(x, w, wg, lb, bfb)


def _hgrn2_kernel(q_ref, lf_ref, k_ref, v_ref, z_ref, g_ref, o_ref, st_ref, *, T):
    si = pl.program_id(1)

    @pl.when(si == 0)
    def _():
        st_ref[...] = jnp.zeros_like(st_ref)

    L = HG_CHUNK
    nch = T // L
    row = lax.broadcasted_iota(jnp.int32, (T, T), 0)
    col = lax.broadcasted_iota(jnp.int32, (T, T), 1)
    same_chunk_causal = ((row // L) == (col // L)) & (col <= row)
    tri = jnp.where(same_chunk_causal, 1.0, 0.0).astype(BF16)
    last = jnp.where(((row // L) == (col // L)), 1.0, 0.0).astype(BF16)

    for h in range(HEADS):
        sl = slice(h * HEAD_DK, (h + 1) * HEAD_DK)
        lf = lf_ref[0, :, sl]
        lf_hi = lf.astype(BF16)
        lf_lo = (lf - lf_hi.astype(F32)).astype(BF16)
        b = (jnp.dot(tri, lf_hi, preferred_element_type=F32)
             + jnp.dot(tri, lf_lo, preferred_element_type=F32))
        btot = (jnp.dot(last, lf_hi, preferred_element_type=F32)
                + jnp.dot(last, lf_lo, preferred_element_type=F32))
        q = q_ref[0, :, sl].astype(F32)
        k = k_ref[0, :, sl].astype(F32)
        v = v_ref[0, :, sl]
        q_dec = (q * jnp.exp(b)).astype(BF16)
        k_inv = (k * jnp.exp(-b)).astype(BF16)
        k_end = (k * jnp.exp(btot - b)).astype(BF16)
        dec_tot = jnp.exp(btot)

        a = lax.dot_general(q_dec, k_inv, (((1,), (1,)), ((), ())), preferred_element_type=F32)
        a = jnp.where(same_chunk_causal, a, 0.0).astype(BF16)
        o = jnp.dot(a, v, preferred_element_type=F32)

        st = st_ref[h]
        inter = []
        for c in range(nch):
            rs = slice(c * L, (c + 1) * L)
            inter.append(lax.dot_general(q_dec[rs], st.astype(BF16), (((1,), (1,)), ((), ())),
                                         preferred_element_type=F32))
            upd = lax.dot_general(v[rs], k_end[rs], (((0,), (0,)), ((), ())),
                                  preferred_element_type=F32)
            st = st * dec_tot[c * L:c * L + 1, :] + upd
        st_ref[h] = st
        o = o + jnp.concatenate(inter, axis=0)

        ms = jnp.mean(o * o, axis=-1, keepdims=True)
        o = o * lax.rsqrt(ms + RMS_EPS) * g_ref[:, sl]
        o_ref[0, :, sl] = (o * z_ref[0, :, sl].astype(F32)).astype(BF16)


def _hgrn2(q, lf, k, v, z, g, *, T):
    B, S, W = q.shape
    tok = pl.BlockSpec((1, T, W), lambda b, s: (b, s, 0))
    return pl.pallas_call(
        functools.partial(_hgrn2_kernel, T=T),
        grid=(B, S // T),
        in_specs=[tok, tok, tok, tok, tok, pl.BlockSpec((1, W), lambda b, s: (0, 0))],
        out_specs=tok,
        out_shape=jax.ShapeDtypeStruct((B, S, W), BF16),
        scratch_shapes=[pltpu.VMEM((HEADS, HEAD_DK, HEAD_DK), F32)],
        compiler_params=_cparams(("parallel", "arbitrary")),
        name="hgrn2",
    )(q, lf, k, v, z, g)


def _fox_kernel(q_ref, k_ref, v_ref, c_ref, z_ref, o_ref, vaug_ref, m_ref, acc_ref, *, tq, tk):
    h = pl.program_id(1)
    qi = pl.program_id(2)

    @pl.when(qi == 0)
    def _():
        vaug_ref[:, :HEAD_DK] = v_ref[0]
        vaug_ref[:, HEAD_DK:] = jnp.ones((vaug_ref.shape[0], HEAD_DK), BF16)

    per_q = tq // tk

    def c_row(j):
        return c_ref[0, j, pl.ds(h, 1), :] * LOG2E

    cref = c_row(qi * per_q)[:, :1]
    m_ref[...] = jnp.full_like(m_ref, -jnp.inf)
    acc_ref[...] = jnp.zeros_like(acc_ref)

    def chunk(j, r0, masked):
        ks = pl.multiple_of(j * tk, tk)
        s = lax.dot_general(q_ref[0, r0:, :], k_ref[0, pl.ds(ks, tk), :], (((1,), (1,)), ((), ())),
                            preferred_element_type=F32)
        s = s + (cref - c_row(j))
        if masked:
            row = lax.broadcasted_iota(jnp.int32, s.shape, 0)
            col = lax.broadcasted_iota(jnp.int32, s.shape, 1)
            s = jnp.where(col <= row, s, -jnp.inf)
        m_old = m_ref[r0:, :]
        m_new = jnp.maximum(m_old, jnp.max(s, axis=-1, keepdims=True))
        alpha = jnp.exp2(m_old - m_new)
        p = jnp.exp2(s - jnp.tile(m_new, (1, tk // 128))).astype(BF16)
        acc_ref[r0:, :] = (jnp.tile(alpha, (1, 2)) * acc_ref[r0:, :]
                           + jnp.dot(p, vaug_ref[pl.ds(ks, tk), :], preferred_element_type=F32))
        m_ref[r0:, :] = m_new

    def body(i, carry):
        for u in range(per_q):
            chunk(i * per_q + u, 0, False)
        return carry

    lax.fori_loop(0, qi, body, 0)
    for u in range(per_q):
        chunk(qi * per_q + u, u * tk, True)
    acc = acc_ref[...]
    o = acc[:, :HEAD_DK] / acc[:, HEAD_DK:]
    o_ref[0] = (o * z_ref[0].astype(F32)).astype(BF16)


def _fox(q, k, v, c, z, *, tq, tk):
    B, S, W = q.shape
    nq, nk = S // tq, S // tk
    c4 = c.reshape(B, GATE_ROWS, nk, tk).transpose(0, 2, 1, 3)
    qspec = pl.BlockSpec((1, tq, HEAD_DK), lambda b, h, i: (b, i, h))
    kspec = pl.BlockSpec((1, S, HEAD_DK), lambda b, h, i: (b, 0, h))
    return pl.pallas_call(
        functools.partial(_fox_kernel, tq=tq, tk=tk),
        grid=(B, HEADS, nq),
        in_specs=[qspec, kspec, kspec,
                  pl.BlockSpec((1, nk, GATE_ROWS, tk), lambda b, h, i: (b, 0, 0, 0)),
                  qspec],
        out_specs=qspec,
        out_shape=jax.ShapeDtypeStruct((B, S, W), BF16),
        scratch_shapes=[pltpu.VMEM((S, 2 * HEAD_DK), BF16), pltpu.VMEM((tq, 128), F32),
                        pltpu.VMEM((tq, 2 * HEAD_DK), F32)],
        compiler_params=_cparams(("parallel", "parallel", "arbitrary")),
        name="fox_attention",
    )(q, k, v, c4, z)


def _outproj_ln_kernel(*refs, n_act):
    act_refs = refs[:n_act]
    w_ref, x_ref, g_ref, b_ref, o_ref = refs[n_act:]
    y = ALPHA * x_ref[...]
    off = 0
    for a_ref in act_refs:
        wd = a_ref.shape[1]
        y = y + jnp.dot(a_ref[...], w_ref[off:off + wd, :], preferred_element_type=F32)
        off += wd
    mu = jnp.mean(y, axis=-1, keepdims=True)
    yc = y - mu
    var = jnp.mean(yc * yc, axis=-1, keepdims=True)
    o_ref[...] = yc * lax.rsqrt(var + LN_EPS) * g_ref[...] + b_ref[...]


def _outproj_ln(acts, w, x, g, b, *, tm):
    N, D = x.shape
    tok = lambda wd: pl.BlockSpec((tm, wd), lambda i: (i, 0))
    const = lambda shape: pl.BlockSpec(shape, lambda i: (0,) * len(shape))
    return pl.pallas_call(
        functools.partial(_outproj_ln_kernel, n_act=len(acts)),
        grid=(N // tm,),
        in_specs=[tok(a.shape[1]) for a in acts] + [const(w.shape), tok(D), const(g.shape), const(b.shape)],
        out_specs=tok(D),
        out_shape=jax.ShapeDtypeStruct((N, D), F32),
        compiler_params=_cparams(("parallel",)),
        name="outproj_ln",
    )(*acts, w, x, g, b)


def _inproj_c_kernel(x_ref, w_ref, wg_ref, cw_ref, cb_ref, gb_ref,
                     q_ref, k_ref, v_ref, og_ref, z_ref, gt_ref,
                     ubuf_ref, *, tm, kw, vw, k_scale):
    si = pl.program_id(1)
    xb = x_ref[0].astype(BF16)

    g = lax.dot_general(wg_ref[...], xb, (((1,), (1,)), ((), ())), preferred_element_type=F32) + gb_ref[...]
    rowi = lax.broadcasted_iota(jnp.int32, g.shape, 0)
    b = _cumsum_lanes(_log_sigmoid(g), seg=ML_CHUNK)
    gt_ref[0] = jnp.where(rowi < HEADS, g - pltpu.roll(b, HEADS, 0), b)

    def mm(lo, hi):
        return jnp.dot(xb, w_ref[:, lo:hi], preferred_element_type=F32)

    pad = 8

    @pl.when(si == 0)
    def _():
        ubuf_ref[0:pad, :] = jnp.zeros((pad, 2 * kw), F32)

    cwid = 256
    for c0 in range(0, 2 * kw, cwid):
        cs = slice(c0, c0 + cwid)
        ubuf_ref[pad:pad + tm, cs] = mm(c0, c0 + cwid)
        acc = cb_ref[:, cs]
        for j in range(ML_CONV):
            st = pad - (ML_CONV - 1) + j
            acc = acc + cw_ref[j:j + 1, cs] * ubuf_ref[st:st + tm, cs]
        ubuf_ref[0:pad, cs] = ubuf_ref[tm:tm + pad, cs]
        if c0 < kw:
            q_ref[0, :, cs] = _silu(acc).astype(BF16)
        else:
            k_ref[0, :, c0 - kw:c0 - kw + cwid] = (_silu(acc) * k_scale).astype(BF16)
    o0 = 2 * kw
    for c0 in range(0, vw, cwid):
        cs = slice(c0, c0 + cwid)
        v_ref[0, :, cs] = mm(o0 + c0, o0 + c0 + cwid).astype(BF16)
        og_ref[0, :, cs] = _sigmoid(mm(o0 + vw + c0, o0 + vw + c0 + cwid)).astype(BF16)
        z_ref[0, :, cs] = _silu(mm(o0 + 2 * vw + c0, o0 + 2 * vw + c0 + cwid)).astype(BF16)


def _inproj_c(x, w, wg, cw, cb, gb, *, tm):
    B, S, D = x.shape
    kw = cw.shape[1] // 2
    vw = (w.shape[1] - 2 * kw) // 3
    const = lambda shape: pl.BlockSpec(shape, lambda b, s: (0,) * len(shape))
    tok = lambda wd: pl.BlockSpec((1, tm, wd), lambda b, s: (b, s, 0))
    return pl.pallas_call(
        functools.partial(_inproj_c_kernel, tm=tm, kw=kw, vw=vw, k_scale=HEAD_DK ** -0.5),
        grid=(B, S // tm),
        in_specs=[pl.BlockSpec((1, tm, D), lambda b, s: (b, s, 0)),
                  const(w.shape), const(wg.shape), const(cw.shape), const(cb.shape), const(gb.shape)],
        out_specs=[tok(kw), tok(kw), tok(vw), tok(vw), tok(vw),
                   pl.BlockSpec((1, GATE_ROWS, tm), lambda b, s: (b, 0, s))],
        out_shape=[jax.ShapeDtypeStruct((B, S, kw), BF16), jax.ShapeDtypeStruct((B, S, kw), BF16),
                   jax.ShapeDtypeStruct((B, S, vw), BF16), jax.ShapeDtypeStruct((B, S, vw), BF16),
                   jax.ShapeDtypeStruct((B, S, vw), BF16), jax.ShapeDtypeStruct((B, GATE_ROWS, S), F32)],
        scratch_shapes=[pltpu.VMEM((tm + 8, 2 * kw), F32)],
        compiler_params=_cparams(("parallel", "arbitrary")),
        name="inproj_c",
    )(x, w, wg, cw, cb, gb)


def _mlstm_kernel(q_ref, k_ref, v_ref, gt_ref, og_ref, z_ref, g_ref, o_ref, c_ref, n_ref, m_ref, *, L):
    @pl.when(pl.program_id(1) == 0)
    def _():
        c_ref[...] = jnp.zeros_like(c_ref)
        n_ref[...] = jnp.zeros_like(n_ref)
        m_ref[...] = jnp.zeros_like(m_ref)

    for h in range(HEADS):
        _mlstm_head(h, q_ref, k_ref, v_ref, gt_ref, og_ref, z_ref, g_ref, o_ref, c_ref, n_ref, m_ref, L=L)


def _mlstm_head(h, q_ref, k_ref, v_ref, gt_ref, og_ref, z_ref, g_ref, o_ref, c_ref, n_ref, m_ref, *, L):
    ks = slice(h * HEAD_DK, (h + 1) * HEAD_DK)
    vs = slice(h * ML_DV, (h + 1) * ML_DV)
    q = q_ref[0, :, ks]
    k = k_ref[0, :, ks]
    v = v_ref[0, :, vs]
    a_row = gt_ref[0, h:h + 1, :]
    b_row = gt_ref[0, HEADS + h:HEADS + h + 1, :]
    m_prev = m_ref[h:h + 1, :1]

    row = lax.broadcasted_iota(jnp.int32, (L, L), 0)
    col = lax.broadcasted_iota(jnp.int32, (L, L), 1)
    tril = col <= row
    M_col = jnp.maximum(m_prev, jnp.max(jnp.where(tril, a_row, -jnp.inf), axis=-1, keepdims=True))
    b_col = jnp.sum(jnp.where(row == col, b_row, 0.0), axis=-1, keepdims=True)
    w = jnp.where(tril, jnp.exp(a_row - M_col), 0.0)
    s_inter = jnp.exp(m_prev - M_col)

    qk = lax.dot_general(q, k, (((1,), (1,)), ((), ())), preferred_element_type=F32) * w
    num = (jnp.dot(qk.astype(BF16), v, preferred_element_type=F32)
           + s_inter * jnp.dot(q, c_ref[h].astype(BF16), preferred_element_type=F32))
    den = (jnp.sum(qk, axis=-1, keepdims=True)
           + s_inter * jnp.sum(q.astype(F32) * n_ref[h:h + 1, :], axis=-1, keepdims=True))
    hval = num / jnp.maximum(jnp.abs(den), jnp.exp(-(b_col + M_col)))

    M_last = jnp.maximum(m_prev, jnp.max(a_row, axis=-1, keepdims=True))
    wk_row = jnp.exp(a_row - M_last)
    decay = jnp.exp(m_prev - M_last)
    wk_col = jnp.sum(jnp.where(row == col, wk_row, 0.0), axis=-1, keepdims=True)
    kw = (k.astype(F32) * wk_col).astype(BF16)
    c_ref[h] = decay * c_ref[h] + lax.dot_general(kw, v, (((0,), (0,)), ((), ())),
                                                  preferred_element_type=F32)
    wk8 = jnp.broadcast_to(wk_row, (8, L)).astype(BF16)
    n_ref[h:h + 1, :] = decay * n_ref[h:h + 1, :] + jnp.dot(wk8, k, preferred_element_type=F32)[:1]
    m_ref[h:h + 1, :] = jnp.broadcast_to(b_row[:, L - 1:] + M_last, (1, m_ref.shape[1]))

    ht = og_ref[0, :, vs].astype(F32) * hval
    ms = jnp.mean(ht * ht, axis=-1, keepdims=True)
    ht = ht * lax.rsqrt(ms + RMS_EPS) * g_ref[:, vs]
    o_ref[0, :, vs] = (ht * z_ref[0, :, vs].astype(F32)).astype(BF16)


def _mlstm(q, k, v, gt, og, z, g, *, L):
    B, S, KW = q.shape
    W = v.shape[2]
    kspec = pl.BlockSpec((1, L, KW), lambda b, c: (b, c, 0))
    vspec = pl.BlockSpec((1, L, W), lambda b, c: (b, c, 0))
    return pl.pallas_call(
        functools.partial(_mlstm_kernel, L=L),
        grid=(B, S // L),
        in_specs=[kspec, kspec, vspec,
                  pl.BlockSpec((1, GATE_ROWS, L), lambda b, c: (b, 0, c)),
                  vspec, vspec, pl.BlockSpec((1, W), lambda b, c: (0, 0))],
        out_specs=vspec,
        out_shape=jax.ShapeDtypeStruct((B, S, W), BF16),
        scratch_shapes=[pltpu.VMEM((HEADS, HEAD_DK, ML_DV), F32), pltpu.VMEM((GATE_ROWS, HEAD_DK), F32),
                        pltpu.VMEM((GATE_ROWS, 128), F32)],
        compiler_params=_cparams(("parallel", "arbitrary")),
        name="mlstm",
    )(q, k, v, gt, og, z, g)


def _gate_rows(w_cols):
    wt = w_cols.T
    return jnp.pad(wt, ((0, GATE_ROWS - wt.shape[0]), (0, 0))).astype(BF16)


def kernel(x, hgrn_lb_logits, ab_w_in, ab_fox_bf, ab_hgrn_norm_g, ab_w_out, c_w_in,
           c_conv_w, c_conv_b, c_bi, c_bf, c_norm_g, c_w_out, ln_g, ln_b):
    B, S, D = x.shape
    W = HEADS * HEAD_DK
    TM = min(1024, S)

    lb_table = jnp.cumsum(jax.nn.softmax(hgrn_lb_logits.astype(F32), axis=0), axis=0)
    h = x.astype(F32)

    w_in = ab_w_in[0].astype(F32)
    w_main = jnp.concatenate([w_in[:, :7 * W], w_in[:, 7 * W + HEADS:]], axis=1).astype(BF16)
    w_gate = _gate_rows(w_in[:, 7 * W:7 * W + HEADS])
    bfb = jnp.broadcast_to(jnp.pad(ab_fox_bf[0].astype(F32), (0, GATE_ROWS - HEADS))[:, None], (GATE_ROWS, TM))
    hq, lf, hk, hv, hz, fq, fk, fv, fz, c = _inproj_ab(
        h, w_main, w_gate, lb_table[0][None, :], bfb, tm=TM)
    hg = _hgrn2(hq, lf, hk, hv, hz, ab_hgrn_norm_g[0].astype(F32)[None, :], T=min(256, S))
    fx = _fox(fq, fk, fv, c, fz, tq=min(1024, S), tk=min(512, S))
    h = _outproj_ln([hg.reshape(B * S, W), fx.reshape(B * S, W)], ab_w_out[0].astype(BF16),
                    h.reshape(B * S, D), ln_g[0].astype(F32)[None, :], ln_b[0].astype(F32)[None, :],
                    tm=TM).reshape(B, S, D)

    w_in = c_w_in[0].astype(F32)
    KW, VW = W, HEADS * ML_DV
    o_i = 2 * KW + VW
    w_main = jnp.concatenate([w_in[:, :o_i], w_in[:, o_i + 2 * HEADS:]], axis=1).astype(BF16)
    w_gate = _gate_rows(w_in[:, o_i:o_i + 2 * HEADS])
    gb = jnp.broadcast_to(jnp.concatenate([c_bi[0], c_bf[0]]).astype(F32)[:, None], (GATE_ROWS, TM))
    mq, mk, mv, og, mz, gt = _inproj_c(h, w_main, w_gate, c_conv_w[0].astype(F32),
                                       c_conv_b[0].astype(F32)[None, :], gb, tm=TM)
    ht = _mlstm(mq, mk, mv, gt, og, mz, c_norm_g[0].astype(F32)[None, :], L=ML_CHUNK)
    h = _outproj_ln([ht.reshape(B * S, VW)], c_w_out[0].astype(BF16), h.reshape(B * S, D),
                    ln_g[1].astype(F32)[None, :], ln_b[1].astype(F32)[None, :], tm=TM).reshape(B, S, D)
    return h.astype(x.dtype)
```

```python
import functools
import math

import jax
import jax.numpy as jnp
from jax import lax
from jax.experimental import pallas as pl
from jax.experimental.pallas import tpu as pltpu

F32 = jnp.float32
BF16 = jnp.bfloat16

DEPTH = 2
ALPHA = (2 * DEPTH) ** 0.25
LN_EPS = 1e-5
RMS_EPS = 1e-6

HEADS = 4
HEAD_DK = 128
ML_DV = 256
HG_CHUNK = 32
ML_CONV = 4
ML_CHUNK = 256
GATE_ROWS = 8

VMEM_LIMIT = 56 * 1024 * 1024
LOG2E = math.log2(math.e)


def _cparams(sem):
    return pltpu.CompilerParams(dimension_semantics=sem, vmem_limit_bytes=VMEM_LIMIT)


def _sigmoid(z):
    return 0.5 + 0.5 * jnp.tanh(0.5 * z)


def _silu(z):
    hz = 0.5 * z
    return hz + hz * jnp.tanh(hz)


def _log_sigmoid(z):
    return jnp.minimum(z, 0.0) - jnp.log(1.0 + jnp.exp(-jnp.abs(z)))


def _cumsum_lanes(x, seg=None):
    seg = seg or x.shape[-1]
    pos = lax.broadcasted_iota(jnp.int32, x.shape, x.ndim - 1) % seg
    sh = 1
    while sh < seg:
        x = x + jnp.where(pos >= sh, pltpu.roll(x, sh, x.ndim - 1), 0.0)
        sh *= 2
    return x


def _inproj_ab_kernel(x_ref, w_ref, wg_ref, lb_ref, bf_ref,
                      q_ref, lf_ref, k_ref, v_ref, z_ref, fq_ref, fk_ref, fv_ref, fz_ref, c_ref,
                      carry_ref, *, width, fq_scale):
    si = pl.program_id(1)
    xb = x_ref[0].astype(BF16)

    g = lax.dot_general(wg_ref[...], xb, (((1,), (1,)), ((), ())), preferred_element_type=F32)
    ls = _log_sigmoid(g + bf_ref[...])

    @pl.when(si == 0)
    def _():
        carry_ref[...] = jnp.zeros_like(carry_ref)

    c = _cumsum_lanes(ls) + carry_ref[:, :1]
    c_ref[0] = c
    carry_ref[...] = jnp.broadcast_to(c[:, -1:], carry_ref.shape)

    def mm(j):
        return jnp.dot(xb, w_ref[:, j * width:(j + 1) * width], preferred_element_type=F32)

    q_ref[0] = mm(0).astype(BF16)
    lb = lb_ref[...]
    f = lb + (1.0 - lb) * _sigmoid(mm(1))
    lf_ref[0] = jnp.log(f)
    k_ref[0] = (1.0 - f).astype(BF16)
    v_ref[0] = mm(2).astype(BF16)
    z_ref[0] = _silu(mm(3)).astype(BF16)
    fq_ref[0] = (mm(4) * fq_scale).astype(BF16)
    fk_ref[0] = mm(5).astype(BF16)
    fv_ref[0] = mm(6).astype(BF16)
    fz_ref[0] = _silu(mm(7)).astype(BF16)


def _inproj_ab(x, w, wg, lb, bfb, *, tm):
    B, S, D = x.shape
    width = w.shape[1] // 8
    wide = lambda dt: jax.ShapeDtypeStruct((B, S, width), dt)
    tok = pl.BlockSpec((1, tm, width), lambda b, s: (b, s, 0))
    const = lambda shape: pl.BlockSpec(shape, lambda b, s: (0,) * len(shape))
    return pl.pallas_call(
        functools.partial(_inproj_ab_kernel, width=width, fq_scale=HEAD_DK ** -0.5 * LOG2E),
        grid=(B, S // tm),
        in_specs=[pl.BlockSpec((1, tm, D), lambda b, s: (b, s, 0)),
                  const(w.shape), const(wg.shape), const(lb.shape), const(bfb.shape)],
        out_specs=[tok, tok, tok, tok, tok, tok, tok, tok, tok,
                   pl.BlockSpec((1, GATE_ROWS, tm), lambda b, s: (b, 0, s))],
        out_shape=[wide(BF16), wide(F32), wide(BF16), wide(BF16), wide(BF16),
                   wide(BF16), wide(BF16), wide(BF16), wide(BF16),
                   jax.ShapeDtypeStruct((B, GATE_ROWS, S), F32)],
        scratch_shapes=[pltpu.VMEM((GATE_ROWS, 128), F32)],
        compiler_params=_cparams(("parallel", "arbitrary")),
        name="inproj_ab",
    )(x, w, wg, lb, bfb)


def _hgrn2_kernel(q_ref, lf_ref, k_ref, v_ref, z_ref, g_ref, o_ref, st_ref, *, T):
    si = pl.program_id(1)

    @pl.when(si == 0)
    def _():
        st_ref[...] = jnp.zeros_like(st_ref)

    L = HG_CHUNK
    BL = 2 * L
    row = lax.broadcasted_iota(jnp.int32, (T, T), 0)
    col = lax.broadcasted_iota(jnp.int32, (T, T), 1)
    rc, cc = row // L, col // L
    same_chunk = rc == cc
    same_chunk_causal = same_chunk & (col <= row)
    prev_chunk_in_block = (rc == cc + 1) & (rc % 2 == 1)
    tri = jnp.where(same_chunk_causal, 1.0, 0.0).astype(BF16)
    last = jnp.where(same_chunk, 1.0, 0.0).astype(BF16)
    odd_row = (lax.broadcasted_iota(jnp.int32, (T, HEAD_DK), 0) // L) % 2 == 1

    lf_all = lf_ref[0]
    lf_hi = lf_all.astype(BF16)
    lf_lo = (lf_all - lf_hi.astype(F32)).astype(BF16)
    b_all = (jnp.dot(tri, lf_hi, preferred_element_type=F32)
             + jnp.dot(tri, lf_lo, preferred_element_type=F32))
    btot_all = (jnp.dot(last, lf_hi, preferred_element_type=F32)
                + jnp.dot(last, lf_lo, preferred_element_type=F32))

    for h in range(HEADS):
        sl = slice(h * HEAD_DK, (h + 1) * HEAD_DK)
        b = b_all[:, sl]
        btot = btot_all[:, sl]
        q = q_ref[0, :, sl].astype(F32)
        k = k_ref[0, :, sl].astype(F32)
        v = v_ref[0, :, sl]
        dec_tot = jnp.exp(btot)
        dec_prev = pltpu.roll(dec_tot, L, 0)
        dec_next = pltpu.roll(dec_tot, T - L, 0)
        q_dec32 = q * jnp.exp(b)
        k_end32 = k * jnp.exp(btot - b)
        q_dec = q_dec32.astype(BF16)
        k_inv = (k * jnp.exp(-b)).astype(BF16)
        k_end = k_end32.astype(BF16)
        q_blk = jnp.where(odd_row, q_dec32 * dec_prev, q_dec32).astype(BF16)
        k_blk = jnp.where(odd_row, k_end32, k_end32 * dec_next).astype(BF16)
        dec_blk = dec_tot * dec_next

        nt = (((1,), (1,)), ((), ()))
        a_in = lax.dot_general(q_dec, k_inv, nt, preferred_element_type=F32)
        a_x = lax.dot_general(q_dec, k_end, nt, preferred_element_type=F32)
        a = jnp.where(same_chunk_causal, a_in, jnp.where(prev_chunk_in_block, a_x, 0.0)).astype(BF16)
        o = jnp.dot(a, v, preferred_element_type=F32)

        st = st_ref[h]
        inter = []
        for c in range(T // BL):
            rs = slice(c * BL, (c + 1) * BL)
            inter.append(lax.dot_general(q_blk[rs], st.astype(BF16), nt, preferred_element_type=F32))
            upd = lax.dot_general(v[rs], k_blk[rs], (((0,), (0,)), ((), ())),
                                  preferred_element_type=F32)
            st = st * dec_blk[c * BL:c * BL + 1, :] + upd
        st_ref[h] = st
        o = o + jnp.concatenate(inter, axis=0)

        ms = jnp.mean(o * o, axis=-1, keepdims=True)
        o = o * lax.rsqrt(ms + RMS_EPS) * g_ref[:, sl]
        o_ref[0, :, sl] = (o * z_ref[0, :, sl].astype(F32)).astype(BF16)


def _hgrn2(q, lf, k, v, z, g, *, T):
    B, S, W = q.shape
    tok = pl.BlockSpec((1, T, W), lambda b, s: (b, s, 0))
    return pl.pallas_call(
        functools.partial(_hgrn2_kernel, T=T),
        grid=(B, S // T),
        in_specs=[tok, tok, tok, tok, tok, pl.BlockSpec((1, W), lambda b, s: (0, 0))],
        out_specs=tok,
        out_shape=jax.ShapeDtypeStruct((B, S, W), BF16),
        scratch_shapes=[pltpu.VMEM((HEADS, HEAD_DK, HEAD_DK), F32)],
        compiler_params=_cparams(("parallel", "arbitrary")),
        name="hgrn2",
    )(q, lf, k, v, z, g)


def _fox_kernel(q_ref, k_ref, v_ref, c_ref, z_ref, o_ref, vaug_ref, m_ref, acc_ref, *, tq, tk):
    h = pl.program_id(1)
    qi = pl.program_id(2)

    @pl.when(qi == 0)
    def _():
        vaug_ref[:, :HEAD_DK] = v_ref[0]
        vaug_ref[:, HEAD_DK:] = jnp.ones((vaug_ref.shape[0], HEAD_DK), BF16)

    per_q = tq // tk

    def c_row(j):
        return c_ref[0, j, pl.ds(h, 1), :] * LOG2E

    cref = c_row(qi * per_q)[:, :1]
    m_ref[...] = jnp.full_like(m_ref, -jnp.inf)
    acc_ref[...] = jnp.zeros_like(acc_ref)

    def chunk(j, r0, masked):
        ks = pl.multiple_of(j * tk, tk)
        s = lax.dot_general(q_ref[0, r0:, :], k_ref[0, pl.ds(ks, tk), :], (((1,), (1,)), ((), ())),
                            preferred_element_type=F32)
        s = s + (cref - c_row(j))
        if masked:
            row = lax.broadcasted_iota(jnp.int32, s.shape, 0)
            col = lax.broadcasted_iota(jnp.int32, s.shape, 1)
            s = jnp.where(col <= row, s, -jnp.inf)
        m_old = m_ref[r0:, :]
        m_new = jnp.maximum(m_old, jnp.max(s, axis=-1, keepdims=True))
        alpha = jnp.exp2(m_old - m_new)
        p = jnp.exp2(s - jnp.tile(m_new, (1, tk // 128))).astype(BF16)
        acc_ref[r0:, :] = (jnp.tile(alpha, (1, 2)) * acc_ref[r0:, :]
                           + jnp.dot(p, vaug_ref[pl.ds(ks, tk), :], preferred_element_type=F32))
        m_ref[r0:, :] = m_new

    def body(i, carry):
        for u in range(per_q):
            chunk(i * per_q + u, 0, False)
        return carry

    lax.fori_loop(0, qi, body, 0)
    for u in range(per_q):
        chunk(qi * per_q + u, u * tk, True)
    acc = acc_ref[...]
    o = acc[:, :HEAD_DK] / acc[:, HEAD_DK:]
    o_ref[0] = (o * z_ref[0].astype(F32)).astype(BF16)


def _fox(q, k, v, c, z, *, tq, tk):
    B, S, W = q.shape
    nq, nk = S // tq, S // tk
    c4 = c.reshape(B, GATE_ROWS, nk, tk).transpose(0, 2, 1, 3)
    qspec = pl.BlockSpec((1, tq, HEAD_DK), lambda b, h, i: (b, i, h))
    kspec = pl.BlockSpec((1, S, HEAD_DK), lambda b, h, i: (b, 0, h))
    return pl.pallas_call(
        functools.partial(_fox_kernel, tq=tq, tk=tk),
        grid=(B, HEADS, nq),
        in_specs=[qspec, kspec, kspec,
                  pl.BlockSpec((1, nk, GATE_ROWS, tk), lambda b, h, i: (b, 0, 0, 0)),
                  qspec],
        out_specs=qspec,
        out_shape=jax.ShapeDtypeStruct((B, S, W), BF16),
        scratch_shapes=[pltpu.VMEM((S, 2 * HEAD_DK), BF16), pltpu.VMEM((tq, 128), F32),
                        pltpu.VMEM((tq, 2 * HEAD_DK), F32)],
        compiler_params=_cparams(("parallel", "parallel", "arbitrary")),
        name="fox_attention",
    )(q, k, v, c4, z)


def _outproj_ln_kernel(*refs, n_act):
    act_refs = refs[:n_act]
    w_ref, x_ref, g_ref, b_ref, o_ref = refs[n_act:]
    y = ALPHA * x_ref[...]
    off = 0
    for a_ref in act_refs:
        wd = a_ref.shape[1]
        y = y + jnp.dot(a_ref[...], w_ref[off:off + wd, :], preferred_element_type=F32)
        off += wd
    mu = jnp.mean(y, axis=-1, keepdims=True)
    yc = y - mu
    var = jnp.mean(yc * yc, axis=-1, keepdims=True)
    o_ref[...] = yc * lax.rsqrt(var + LN_EPS) * g_ref[...] + b_ref[...]


def _outproj_ln(acts, w, x, g, b, *, tm):
    N, D = x.shape
    tok = lambda wd: pl.BlockSpec((tm, wd), lambda i: (i, 0))
    const = lambda shape: pl.BlockSpec(shape, lambda i: (0,) * len(shape))
    return pl.pallas_call(
        functools.partial(_outproj_ln_kernel, n_act=len(acts)),
        grid=(N // tm,),
        in_specs=[tok(a.shape[1]) for a in acts] + [const(w.shape), tok(D), const(g.shape), const(b.shape)],
        out_specs=tok(D),
        out_shape=jax.ShapeDtypeStruct((N, D), F32),
        compiler_params=_cparams(("parallel",)),
        name="outproj_ln",
    )(*acts, w, x, g, b)


def _inproj_c_kernel(x_ref, w_ref, wg_ref, cw_ref, cb_ref, gb_ref,
                     q_ref, k_ref, v_ref, og_ref, z_ref, gt_ref,
                     ubuf_ref, *, tm, kw, vw, k_scale):
    si = pl.program_id(1)
    xb = x_ref[0].astype(BF16)

    g = lax.dot_general(wg_ref[...], xb, (((1,), (1,)), ((), ())), preferred_element_type=F32) + gb_ref[...]
    rowi = lax.broadcasted_iota(jnp.int32, g.shape, 0)
    b = _cumsum_lanes(_log_sigmoid(g), seg=ML_CHUNK)
    gt_ref[0] = jnp.where(rowi < HEADS, g - pltpu.roll(b, HEADS, 0), b)

    def mm(lo, hi):
        return jnp.dot(xb, w_ref[:, lo:hi], preferred_element_type=F32)

    pad = 8

    @pl.when(si == 0)
    def _():
        ubuf_ref[0:pad, :] = jnp.zeros((pad, 2 * kw), F32)

    cwid = 256
    for c0 in range(0, 2 * kw, cwid):
        cs = slice(c0, c0 + cwid)
        ubuf_ref[pad:pad + tm, cs] = mm(c0, c0 + cwid)
        acc = cb_ref[:, cs]
        for j in range(ML_CONV):
            st = pad - (ML_CONV - 1) + j
            acc = acc + cw_ref[j:j + 1, cs] * ubuf_ref[st:st + tm, cs]
        ubuf_ref[0:pad, cs] = ubuf_ref[tm:tm + pad, cs]
        if c0 < kw:
            q_ref[0, :, cs] = _silu(acc).astype(BF16)
        else:
            k_ref[0, :, c0 - kw:c0 - kw + cwid] = (_silu(acc) * k_scale).astype(BF16)
    o0 = 2 * kw
    for c0 in range(0, vw, cwid):
        cs = slice(c0, c0 + cwid)
        v_ref[0, :, cs] = mm(o0 + c0, o0 + c0 + cwid).astype(BF16)
        og_ref[0, :, cs] = _sigmoid(mm(o0 + vw + c0, o0 + vw + c0 + cwid)).astype(BF16)
        z_ref[0, :, cs] = _silu(mm(o0 + 2 * vw + c0, o0 + 2 * vw + c0 + cwid)).astype(BF16)


def _inproj_c(x, w, wg, cw, cb, gb, *, tm):
    B, S, D = x.shape
    kw = cw.shape[1] // 2
    vw = (w.shape[1] - 2 * kw) // 3
    const = lambda shape: pl.BlockSpec(shape, lambda b, s: (0,) * len(shape))
    tok = lambda wd: pl.BlockSpec((1, tm, wd), lambda b, s: (b, s, 0))
    return pl.pallas_call(
        functools.partial(_inproj_c_kernel, tm=tm, kw=kw, vw=vw, k_scale=HEAD_DK ** -0.5),
        grid=(B, S // tm),
        in_specs=[pl.BlockSpec((1, tm, D), lambda b, s: (b, s, 0)),
                  const(w.shape), const(wg.shape), const(cw.shape), const(cb.shape), const(gb.shape)],
        out_specs=[tok(kw), tok(kw), tok(vw), tok(vw), tok(vw),
                   pl.BlockSpec((1, GATE_ROWS, tm), lambda b, s: (b, 0, s))],
        out_shape=[jax.ShapeDtypeStruct((B, S, kw), BF16), jax.ShapeDtypeStruct((B, S, kw), BF16),
                   jax.ShapeDtypeStruct((B, S, vw), BF16), jax.ShapeDtypeStruct((B, S, vw), BF16),
                   jax.ShapeDtypeStruct((B, S, vw), BF16), jax.ShapeDtypeStruct((B, GATE_ROWS, S), F32)],
        scratch_shapes=[pltpu.VMEM((tm + 8, 2 * kw), F32)],
        compiler_params=_cparams(("parallel", "arbitrary")),
        name="inproj_c",
    )(x, w, wg, cw, cb, gb)


def _mlstm_kernel(q_ref, k_ref, v_ref, gt_ref, og_ref, z_ref, g_ref, o_ref, c_ref, m_ref, *, L):
    @pl.when(pl.program_id(1) == 0)
    def _():
        c_ref[...] = jnp.zeros_like(c_ref)
        m_ref[...] = jnp.zeros_like(m_ref)

    for h in range(HEADS):
        _mlstm_head(h, q_ref, k_ref, v_ref, gt_ref, og_ref, z_ref, g_ref, o_ref, c_ref, m_ref, L=L)


def _mlstm_head(h, q_ref, k_ref, v_ref, gt_ref, og_ref, z_ref, g_ref, o_ref, c_ref, m_ref, *, L):
    ks = slice(h * HEAD_DK, (h + 1) * HEAD_DK)
    vs = slice(h * ML_DV, (h + 1) * ML_DV)
    q = q_ref[0, :, ks]
    k = k_ref[0, :, ks]
    v = v_ref[0, :, vs]
    a_row = gt_ref[0, h:h + 1, :]
    b_row = gt_ref[0, HEADS + h:HEADS + h + 1, :]
    m_prev = m_ref[h:h + 1, :1]

    row = lax.broadcasted_iota(jnp.int32, (L, L), 0)
    col = lax.broadcasted_iota(jnp.int32, (L, L), 1)
    tril = col <= row
    M_col = jnp.maximum(m_prev, jnp.max(jnp.where(tril, a_row, -jnp.inf), axis=-1, keepdims=True))
    b_col = jnp.sum(jnp.where(row == col, b_row, 0.0), axis=-1, keepdims=True)
    w = jnp.where(tril, jnp.exp(a_row - M_col), 0.0)
    s_inter = jnp.exp(m_prev - M_col)

    v_ext = jnp.concatenate([v, jnp.ones((L, HEAD_DK), BF16)], axis=1)
    qk = lax.dot_general(q, k, (((1,), (1,)), ((), ())), preferred_element_type=F32) * w
    ne = (jnp.dot(qk.astype(BF16), v_ext, preferred_element_type=F32)
          + s_inter * jnp.dot(q, c_ref[h].astype(BF16), preferred_element_type=F32))
    den = jnp.maximum(jnp.abs(ne[:, ML_DV:]), jnp.exp(-(b_col + M_col)))
    hval = ne[:, :ML_DV] / jnp.tile(den, (1, ML_DV // HEAD_DK))

    M_last = jnp.maximum(m_prev, jnp.max(a_row, axis=-1, keepdims=True))
    wk_row = jnp.exp(a_row - M_last)
    decay = jnp.exp(m_prev - M_last)
    wk_col = jnp.sum(jnp.where(row == col, wk_row, 0.0), axis=-1, keepdims=True)
    kw = (k.astype(F32) * wk_col).astype(BF16)
    c_ref[h] = decay * c_ref[h] + lax.dot_general(kw, v_ext, (((0,), (0,)), ((), ())),
                                                  preferred_element_type=F32)
    m_ref[h:h + 1, :] = jnp.broadcast_to(b_row[:, L - 1:] + M_last, (1, m_ref.shape[1]))

    ht = og_ref[0, :, vs].astype(F32) * hval
    ms = jnp.mean(ht * ht, axis=-1, keepdims=True)
    ht = ht * lax.rsqrt(ms + RMS_EPS) * g_ref[:, vs]
    o_ref[0, :, vs] = (ht * z_ref[0, :, vs].astype(F32)).astype(BF16)


def _mlstm(q, k, v, gt, og, z, g, *, L):
    B, S, KW = q.shape
    W = v.shape[2]
    kspec = pl.BlockSpec((1, L, KW), lambda b, c: (b, c, 0))
    vspec = pl.BlockSpec((1, L, W), lambda b, c: (b, c, 0))
    return pl.pallas_call(
        functools.partial(_mlstm_kernel, L=L),
        grid=(B, S // L),
        in_specs=[kspec, kspec, vspec,
                  pl.BlockSpec((1, GATE_ROWS, L), lambda b, c: (b, 0, c)),
                  vspec, vspec, pl.BlockSpec((1, W), lambda b, c: (0, 0))],
        out_specs=vspec,
        out_shape=jax.ShapeDtypeStruct((B, S, W), BF16),
        scratch_shapes=[pltpu.VMEM((HEADS, HEAD_DK, ML_DV + HEAD_DK), F32),
                        pltpu.VMEM((GATE_ROWS, 128), F32)],
        compiler_params=_cparams(("parallel", "arbitrary")),
        name="mlstm",
    )(q, k, v, gt, og, z, g)


def _gate_rows(w_cols):
    wt = w_cols.T
    return jnp.pad(wt, ((0, GATE_ROWS - wt.shape[0]), (0, 0))).astype(BF16)


def kernel(x, hgrn_lb_logits, ab_w_in, ab_fox_bf, ab_hgrn_norm_g, ab_w_out, c_w_in,
           c_conv_w, c_conv_b, c_bi, c_bf, c_norm_g, c_w_out, ln_g, ln_b):
    B, S, D = x.shape
    W = HEADS * HEAD_DK
    TM = min(1024, S)

    lb_table = jnp.cumsum(jax.nn.softmax(hgrn_lb_logits.astype(F32), axis=0), axis=0)
    h = x.astype(F32)

    w_in = ab_w_in[0].astype(F32)
    w_main = jnp.concatenate([w_in[:, :7 * W], w_in[:, 7 * W + HEADS:]], axis=1).astype(BF16)
    w_gate = _gate_rows(w_in[:, 7 * W:7 * W + HEADS])
    bfb = jnp.broadcast_to(jnp.pad(ab_fox_bf[0].astype(F32), (0, GATE_ROWS - HEADS))[:, None], (GATE_ROWS, TM))
    hq, lf, hk, hv, hz, fq, fk, fv, fz, c = _inproj_ab(
        h, w_main, w_gate, lb_table[0][None, :], bfb, tm=TM)
    hg = _hgrn2(hq, lf, hk, hv, hz, ab_hgrn_norm_g[0].astype(F32)[None, :], T=min(256, S))
    fx = _fox(fq, fk, fv, c, fz, tq=min(2048, S), tk=min(256, S))
    h = _outproj_ln([hg.reshape(B * S, W), fx.reshape(B * S, W)], ab_w_out[0].astype(BF16),
                    h.reshape(B * S, D), ln_g[0].astype(F32)[None, :], ln_b[0].astype(F32)[None, :],
                    tm=TM).reshape(B, S, D)

    w_in = c_w_in[0].astype(F32)
    KW, VW = W, HEADS * ML_DV
    o_i = 2 * KW + VW
    w_main = jnp.concatenate([w_in[:, :o_i], w_in[:, o_i + 2 * HEADS:]], axis=1).astype(BF16)
    w_gate = _gate_rows(w_in[:, o_i:o_i + 2 * HEADS])
    gb = jnp.broadcast_to(jnp.concatenate([c_bi[0], c_bf[0]]).astype(F32)[:, None], (GATE_ROWS, TM))
    mq, mk, mv, og, mz, gt = _inproj_c(h, w_main, w_gate, c_conv_w[0].astype(F32),
                                       c_conv_b[0].astype(F32)[None, :], gb, tm=TM)
    ht = _mlstm(mq, mk, mv, gt, og, mz, c_norm_g[0].astype(F32)[None, :], L=ML_CHUNK)
    h = _outproj_ln([ht.reshape(B * S, VW)], c_w_out[0].astype(BF16), h.reshape(B * S, D),
                    ln_g[1].astype(F32)[None, :], ln_b[1].astype(F32)[None, :], tm=TM).reshape(B, S, D)
    return h.astype(x.dtype)
```

```python
import functools
import math

import jax
import jax.numpy as jnp
from jax import lax
from jax.experimental import pallas as pl
from jax.experimental.pallas import tpu as pltpu

F32 = jnp.float32
BF16 = jnp.bfloat16

DEPTH = 2
ALPHA = (2 * DEPTH) ** 0.25
LN_EPS = 1e-5
RMS_EPS = 1e-6

HEADS = 4
HEAD_DK = 128
ML_DV = 256
HG_CHUNK = 32
ML_CONV = 4
ML_CHUNK = 256
GATE_ROWS = 8

VMEM_LIMIT = 56 * 1024 * 1024
LOG2E = math.log2(math.e)


def _cparams(sem):
    return pltpu.CompilerParams(dimension_semantics=sem, vmem_limit_bytes=VMEM_LIMIT)


def _sigmoid(z):
    return 0.5 + 0.5 * jnp.tanh(0.5 * z)


def _silu(z):
    hz = 0.5 * z
    return hz + hz * jnp.tanh(hz)


def _log_sigmoid(z):
    return jnp.minimum(z, 0.0) - jnp.log(1.0 + jnp.exp(-jnp.abs(z)))


def _cumsum_lanes(x, seg=None):
    seg = seg or x.shape[-1]
    pos = lax.broadcasted_iota(jnp.int32, x.shape, x.ndim - 1) % seg
    sh = 1
    while sh < seg:
        x = x + jnp.where(pos >= sh, pltpu.roll(x, sh, x.ndim - 1), 0.0)
        sh *= 2
    return x


def _inproj_ab_kernel(x_ref, w_ref, wg_ref, lb_ref, bf_ref,
                      q_ref, lf_ref, k_ref, v_ref, z_ref, fq_ref, fk_ref, fv_ref, fz_ref, c_ref,
                      carry_ref, *, width, fq_scale):
    si = pl.program_id(1)
    xb = x_ref[0].astype(BF16)

    g = lax.dot_general(wg_ref[...], xb, (((1,), (1,)), ((), ())), preferred_element_type=F32)
    ls = _log_sigmoid(g + bf_ref[...])

    @pl.when(si == 0)
    def _():
        carry_ref[...] = jnp.zeros_like(carry_ref)

    c = _cumsum_lanes(ls) + carry_ref[:, :1]
    c_ref[0] = c
    carry_ref[...] = jnp.broadcast_to(c[:, -1:], carry_ref.shape)

    def mm(j):
        return jnp.dot(xb, w_ref[:, j * width:(j + 1) * width], preferred_element_type=F32)

    q_ref[0] = mm(0).astype(BF16)
    lb = lb_ref[...]
    f = lb + (1.0 - lb) * _sigmoid(mm(1))
    lf_ref[0] = jnp.log(f)
    k_ref[0] = (1.0 - f).astype(BF16)
    v_ref[0] = mm(2).astype(BF16)
    z_ref[0] = _silu(mm(3)).astype(BF16)
    fq_ref[0] = (mm(4) * fq_scale).astype(BF16)
    fk_ref[0] = mm(5).astype(BF16)
    fv_ref[0] = mm(6).astype(BF16)
    fz_ref[0] = _silu(mm(7)).astype(BF16)


def _inproj_ab(x, w, wg, lb, bfb, *, tm):
    B, S, D = x.shape
    width = w.shape[1] // 8
    wide = lambda dt: jax.ShapeDtypeStruct((B, S, width), dt)
    tok = pl.BlockSpec((1, tm, width), lambda b, s: (b, s, 0))
    const = lambda shape: pl.BlockSpec(shape, lambda b, s: (0,) * len(shape))
    return pl.pallas_call(
        functools.partial(_inproj_ab_kernel, width=width, fq_scale=HEAD_DK ** -0.5 * LOG2E),
        grid=(B, S // tm),
        in_specs=[pl.BlockSpec((1, tm, D), lambda b, s: (b, s, 0)),
                  const(w.shape), const(wg.shape), const(lb.shape), const(bfb.shape)],
        out_specs=[tok, tok, tok, tok, tok, tok, tok, tok, tok,
                   pl.BlockSpec((1, GATE_ROWS, tm), lambda b, s: (b, 0, s))],
        out_shape=[wide(BF16), wide(F32), wide(BF16), wide(BF16), wide(BF16),
                   wide(BF16), wide(BF16), wide(BF16), wide(BF16),
                   jax.ShapeDtypeStruct((B, GATE_ROWS, S), F32)],
        scratch_shapes=[pltpu.VMEM((GATE_ROWS, 128), F32)],
        compiler_params=_cparams(("parallel", "arbitrary")),
        name="inproj_ab",
    )(x, w, wg, lb, bfb)


def _hgrn2_kernel(q_ref, lf_ref, k_ref, v_ref, z_ref, g_ref, o_ref, st_ref, *, T):
    si = pl.program_id(1)

    @pl.when(si == 0)
    def _():
        st_ref[...] = jnp.zeros_like(st_ref)

    L = HG_CHUNK
    BL = 2 * L
    row = lax.broadcasted_iota(jnp.int32, (T, T), 0)
    col = lax.broadcasted_iota(jnp.int32, (T, T), 1)
    rc, cc = row // L, col // L
    same_chunk = rc == cc
    same_chunk_causal = same_chunk & (col <= row)
    prev_chunk_in_block = (rc == cc + 1) & (rc % 2 == 1)
    tri = jnp.where(same_chunk_causal, 1.0, 0.0).astype(BF16)
    last = jnp.where(same_chunk, 1.0, 0.0).astype(BF16)
    odd_row = (lax.broadcasted_iota(jnp.int32, (T, HEAD_DK), 0) // L) % 2 == 1

    lf_all = lf_ref[0]
    lf_hi = lf_all.astype(BF16)
    lf_lo = (lf_all - lf_hi.astype(F32)).astype(BF16)
    b_all = (jnp.dot(tri, lf_hi, preferred_element_type=F32)
             + jnp.dot(tri, lf_lo, preferred_element_type=F32))
    btot_all = (jnp.dot(last, lf_hi, preferred_element_type=F32)
                + jnp.dot(last, lf_lo, preferred_element_type=F32))

    for h in range(HEADS):
        sl = slice(h * HEAD_DK, (h + 1) * HEAD_DK)
        b = b_all[:, sl]
        btot = btot_all[:, sl]
        q = q_ref[0, :, sl].astype(F32)
        k = k_ref[0, :, sl].astype(F32)
        v = v_ref[0, :, sl]
        dec_tot = jnp.exp(btot)
        dec_prev = pltpu.roll(dec_tot, L, 0)
        dec_next = pltpu.roll(dec_tot, T - L, 0)
        q_dec32 = q * jnp.exp(b)
        k_end32 = k * jnp.exp(btot - b)
        q_dec = q_dec32.astype(BF16)
        k_inv = (k * jnp.exp(-b)).astype(BF16)
        k_end = k_end32.astype(BF16)
        q_blk = jnp.where(odd_row, q_dec32 * dec_prev, q_dec32).astype(BF16)
        k_blk = jnp.where(odd_row, k_end32, k_end32 * dec_next).astype(BF16)
        dec_blk = dec_tot * dec_next

        nt = (((1,), (1,)), ((), ()))
        a_in = lax.dot_general(q_dec, k_inv, nt, preferred_element_type=F32)
        a_x = lax.dot_general(q_dec, k_end, nt, preferred_element_type=F32)
        a = jnp.where(same_chunk_causal, a_in, jnp.where(prev_chunk_in_block, a_x, 0.0)).astype(BF16)
        o = jnp.dot(a, v, preferred_element_type=F32)

        st = st_ref[h]
        inter = []
        for c in range(T // BL):
            rs = slice(c * BL, (c + 1) * BL)
            inter.append(lax.dot_general(q_blk[rs], st.astype(BF16), nt, preferred_element_type=F32))
            upd = lax.dot_general(v[rs], k_blk[rs], (((0,), (0,)), ((), ())),
                                  preferred_element_type=F32)
            st = st * dec_blk[c * BL:c * BL + 1, :] + upd
        st_ref[h] = st
        o = o + jnp.concatenate(inter, axis=0)

        ms = jnp.mean(o * o, axis=-1, keepdims=True)
        o = o * lax.rsqrt(ms + RMS_EPS) * g_ref[:, sl]
        o_ref[0, :, sl] = (o * z_ref[0, :, sl].astype(F32)).astype(BF16)


def _hgrn2(q, lf, k, v, z, g, *, T):
    B, S, W = q.shape
    tok = pl.BlockSpec((1, T, W), lambda b, s: (b, s, 0))
    return pl.pallas_call(
        functools.partial(_hgrn2_kernel, T=T),
        grid=(B, S // T),
        in_specs=[tok, tok, tok, tok, tok, pl.BlockSpec((1, W), lambda b, s: (0, 0))],
        out_specs=tok,
        out_shape=jax.ShapeDtypeStruct((B, S, W), BF16),
        scratch_shapes=[pltpu.VMEM((HEADS, HEAD_DK, HEAD_DK), F32)],
        compiler_params=_cparams(("parallel", "arbitrary")),
        name="hgrn2",
    )(q, lf, k, v, z, g)


def _fox_kernel(q_ref, k_ref, v_ref, c_ref, z_ref, o_ref, vaug_ref, m_ref, acc_ref, *, tq, tk):
    h = pl.program_id(1)
    qi = pl.program_id(2)

    @pl.when(qi == 0)
    def _():
        vaug_ref[:, :HEAD_DK] = v_ref[0]
        vaug_ref[:, HEAD_DK:] = jnp.ones((vaug_ref.shape[0], HEAD_DK), BF16)

    per_q = tq // tk

    def c_row(j):
        return c_ref[0, j, pl.ds(h, 1), :] * LOG2E

    cref = c_row(qi * per_q)[:, :1]
    m_ref[...] = jnp.full_like(m_ref, -jnp.inf)
    acc_ref[...] = jnp.zeros_like(acc_ref)

    def chunk(j, r0, masked):
        ks = pl.multiple_of(j * tk, tk)
        s = lax.dot_general(q_ref[0, r0:, :], k_ref[0, pl.ds(ks, tk), :], (((1,), (1,)), ((), ())),
                            preferred_element_type=F32)
        s = s + (cref - c_row(j))
        if masked:
            row = lax.broadcasted_iota(jnp.int32, s.shape, 0)
            col = lax.broadcasted_iota(jnp.int32, s.shape, 1)
            s = jnp.where(col <= row, s, -jnp.inf)
        m_old = m_ref[r0:, :]
        m_new = jnp.maximum(m_old, jnp.max(s, axis=-1, keepdims=True))
        alpha = jnp.exp2(m_old - m_new)
        p = jnp.exp2(s - jnp.tile(m_new, (1, tk // 128))).astype(BF16)
        acc_ref[r0:, :] = (jnp.tile(alpha, (1, 2)) * acc_ref[r0:, :]
                           + jnp.dot(p, vaug_ref[pl.ds(ks, tk), :], preferred_element_type=F32))
        m_ref[r0:, :] = m_new

    def body(i, carry):
        for u in range(per_q):
            chunk(i * per_q + u, 0, False)
        return carry

    lax.fori_loop(0, qi, body, 0)
    for u in range(per_q):
        chunk(qi * per_q + u, u * tk, True)
    acc = acc_ref[...]
    o = acc[:, :HEAD_DK] / acc[:, HEAD_DK:]
    o_ref[0] = (o * z_ref[0].astype(F32)).astype(BF16)


def _fox(q, k, v, c, z, *, tq, tk):
    B, S, W = q.shape
    nq, nk = S // tq, S // tk
    c4 = c.reshape(B, GATE_ROWS, nk, tk).transpose(0, 2, 1, 3)
    qspec = pl.BlockSpec((1, tq, HEAD_DK), lambda b, h, i: (b, i, h))
    kspec = pl.BlockSpec((1, S, HEAD_DK), lambda b, h, i: (b, 0, h))
    return pl.pallas_call(
        functools.partial(_fox_kernel, tq=tq, tk=tk),
        grid=(B, HEADS, nq),
        in_specs=[qspec, kspec, kspec,
                  pl.BlockSpec((1, nk, GATE_ROWS, tk), lambda b, h, i: (b, 0, 0, 0)),
                  qspec],
        out_specs=qspec,
        out_shape=jax.ShapeDtypeStruct((B, S, W), BF16),
        scratch_shapes=[pltpu.VMEM((S, 2 * HEAD_DK), BF16), pltpu.VMEM((tq, 128), F32),
                        pltpu.VMEM((tq, 2 * HEAD_DK), F32)],
        compiler_params=_cparams(("parallel", "parallel", "arbitrary")),
        name="fox_attention",
    )(q, k, v, c4, z)


def _outproj_ln_kernel(*refs, n_act):
    act_refs = refs[:n_act]
    w_ref, x_ref, g_ref, b_ref, o_ref = refs[n_act:]
    y = ALPHA * x_ref[...]
    off = 0
    for a_ref in act_refs:
        wd = a_ref.shape[1]
        y = y + jnp.dot(a_ref[...], w_ref[off:off + wd, :], preferred_element_type=F32)
        off += wd
    mu = jnp.mean(y, axis=-1, keepdims=True)
    yc = y - mu
    var = jnp.mean(yc * yc, axis=-1, keepdims=True)
    o_ref[...] = yc * lax.rsqrt(var + LN_EPS) * g_ref[...] + b_ref[...]


def _outproj_ln(acts, w, x, g, b, *, tm):
    N, D = x.shape
    tok = lambda wd: pl.BlockSpec((tm, wd), lambda i: (i, 0))
    const = lambda shape: pl.BlockSpec(shape, lambda i: (0,) * len(shape))
    return pl.pallas_call(
        functools.partial(_outproj_ln_kernel, n_act=len(acts)),
        grid=(N // tm,),
        in_specs=[tok(a.shape[1]) for a in acts] + [const(w.shape), tok(D), const(g.shape), const(b.shape)],
        out_specs=tok(D),
        out_shape=jax.ShapeDtypeStruct((N, D), F32),
        compiler_params=_cparams(("parallel",)),
        name="outproj_ln",
    )(*acts, w, x, g, b)


def _inproj_c_kernel(x_ref, w_ref, wg_ref, cw_ref, cb_ref, gb_ref,
                     q_ref, k_ref, v_ref, og_ref, z_ref, gt_ref,
                     ubuf_ref, *, tm, kw, vw, k_scale):
    si = pl.program_id(1)
    xb = x_ref[0].astype(BF16)

    g = lax.dot_general(wg_ref[...], xb, (((1,), (1,)), ((), ())), preferred_element_type=F32) + gb_ref[...]
    rowi = lax.broadcasted_iota(jnp.int32, g.shape, 0)
    b = _cumsum_lanes(_log_sigmoid(g), seg=ML_CHUNK)
    gt_ref[0] = jnp.where(rowi < HEADS, g - pltpu.roll(b, HEADS, 0), b)

    def mm(lo, hi):
        return jnp.dot(xb, w_ref[:, lo:hi], preferred_element_type=F32)

    pad = 8

    @pl.when(si == 0)
    def _():
        ubuf_ref[0:pad, :] = jnp.zeros((pad, 2 * kw), F32)

    cwid = 256
    for c0 in range(0, 2 * kw, cwid):
        cs = slice(c0, c0 + cwid)
        ubuf_ref[pad:pad + tm, cs] = mm(c0, c0 + cwid)
        acc = cb_ref[:, cs]
        for j in range(ML_CONV):
            st = pad - (ML_CONV - 1) + j
            acc = acc + cw_ref[j:j + 1, cs] * ubuf_ref[st:st + tm, cs]
        ubuf_ref[0:pad, cs] = ubuf_ref[tm:tm + pad, cs]
        if c0 < kw:
            q_ref[0, :, cs] = _silu(acc).astype(BF16)
        else:
            k_ref[0, :, c0 - kw:c0 - kw + cwid] = (_silu(acc) * k_scale).astype(BF16)
    o0 = 2 * kw
    for c0 in range(0, vw, cwid):
        cs = slice(c0, c0 + cwid)
        v_ref[0, :, cs] = mm(o0 + c0, o0 + c0 + cwid).astype(BF16)
        og_ref[0, :, cs] = _sigmoid(mm(o0 + vw + c0, o0 + vw + c0 + cwid)).astype(BF16)
        z_ref[0, :, cs] = _silu(mm(o0 + 2 * vw + c0, o0 + 2 * vw + c0 + cwid)).astype(BF16)


def _inproj_c(x, w, wg, cw, cb, gb, *, tm):
    B, S, D = x.shape
    kw = cw.shape[1] // 2
    vw = (w.shape[1] - 2 * kw) // 3
    const = lambda shape: pl.BlockSpec(shape, lambda b, s: (0,) * len(shape))
    tok = lambda wd: pl.BlockSpec((1, tm, wd), lambda b, s: (b, s, 0))
    return pl.pallas_call(
        functools.partial(_inproj_c_kernel, tm=tm, kw=kw, vw=vw, k_scale=HEAD_DK ** -0.5),
        grid=(B, S // tm),
        in_specs=[pl.BlockSpec((1, tm, D), lambda b, s: (b, s, 0)),
                  const(w.shape), const(wg.shape), const(cw.shape), const(cb.shape), const(gb.shape)],
        out_specs=[tok(kw), tok(kw), tok(vw), tok(vw), tok(vw),
                   pl.BlockSpec((1, GATE_ROWS, tm), lambda b, s: (b, 0, s))],
        out_shape=[jax.ShapeDtypeStruct((B, S, kw), BF16), jax.ShapeDtypeStruct((B, S, kw), BF16),
                   jax.ShapeDtypeStruct((B, S, vw), BF16), jax.ShapeDtypeStruct((B, S, vw), BF16),
                   jax.ShapeDtypeStruct((B, S, vw), BF16), jax.ShapeDtypeStruct((B, GATE_ROWS, S), F32)],
        scratch_shapes=[pltpu.VMEM((tm + 8, 2 * kw), F32)],
        compiler_params=_cparams(("parallel", "arbitrary")),
        name="inproj_c",
    )(x, w, wg, cw, cb, gb)


def _mlstm_kernel(q_ref, k_ref, v_ref, gt_ref, og_ref, z_ref, g_ref, o_ref, c_ref, m_ref, *, L):
    @pl.when(pl.program_id(1) == 0)
    def _():
        c_ref[...] = jnp.zeros_like(c_ref)
        m_ref[...] = jnp.zeros_like(m_ref)

    for sub in range(q_ref.shape[1] // L):
        for h in range(HEADS):
            _mlstm_head(h, slice(sub * L, (sub + 1) * L),
                        q_ref, k_ref, v_ref, gt_ref, og_ref, z_ref, g_ref, o_ref, c_ref, m_ref, L=L)


def _mlstm_head(h, rs, q_ref, k_ref, v_ref, gt_ref, og_ref, z_ref, g_ref, o_ref, c_ref, m_ref, *, L):
    row = lax.broadcasted_iota(jnp.int32, (L, L), 0)
    col = lax.broadcasted_iota(jnp.int32, (L, L), 1)
    tril = col <= row
    eye = col == row
    ks = slice(h * HEAD_DK, (h + 1) * HEAD_DK)
    vs = slice(h * ML_DV, (h + 1) * ML_DV)
    q = q_ref[0, rs, ks]
    k = k_ref[0, rs, ks]
    v = v_ref[0, rs, vs]
    a_row = gt_ref[0, h:h + 1, rs]
    b_row = gt_ref[0, HEADS + h:HEADS + h + 1, rs]
    m_prev = m_ref[h:h + 1, :1]

    M_col = jnp.maximum(m_prev, jnp.max(jnp.where(tril, a_row, -jnp.inf), axis=-1, keepdims=True))
    b_col = jnp.sum(jnp.where(eye, b_row, 0.0), axis=-1, keepdims=True)
    w = jnp.where(tril, jnp.exp(a_row - M_col), 0.0)
    s_inter = jnp.exp(m_prev - M_col)

    v_ext = jnp.concatenate([v, jnp.ones((L, HEAD_DK), BF16)], axis=1)
    qk = lax.dot_general(q, k, (((1,), (1,)), ((), ())), preferred_element_type=F32) * w
    ne = (jnp.dot(qk.astype(BF16), v_ext, preferred_element_type=F32)
          + s_inter * jnp.dot(q, c_ref[h].astype(BF16), preferred_element_type=F32))
    den = jnp.maximum(jnp.abs(ne[:, ML_DV:]), jnp.exp(-(b_col + M_col)))
    hval = ne[:, :ML_DV] / jnp.tile(den, (1, ML_DV // HEAD_DK))

    M_last = jnp.maximum(m_prev, jnp.max(a_row, axis=-1, keepdims=True))
    wk_row = jnp.exp(a_row - M_last)
    decay = jnp.exp(m_prev - M_last)
    wk_col = jnp.sum(jnp.where(eye, wk_row, 0.0), axis=-1, keepdims=True)
    kw = (k.astype(F32) * wk_col).astype(BF16)
    c_ref[h] = decay * c_ref[h] + lax.dot_general(kw, v_ext, (((0,), (0,)), ((), ())),
                                                  preferred_element_type=F32)
    m_ref[h:h + 1, :] = jnp.broadcast_to(b_row[:, L - 1:] + M_last, (1, m_ref.shape[1]))

    ht = og_ref[0, rs, vs].astype(F32) * hval
    ms = jnp.mean(ht * ht, axis=-1, keepdims=True)
    ht = ht * lax.rsqrt(ms + RMS_EPS) * g_ref[:, vs]
    o_ref[0, rs, vs] = (ht * z_ref[0, rs, vs].astype(F32)).astype(BF16)


def _mlstm(q, k, v, gt, og, z, g, *, L, T):
    B, S, KW = q.shape
    W = v.shape[2]
    kspec = pl.BlockSpec((1, T, KW), lambda b, c: (b, c, 0))
    vspec = pl.BlockSpec((1, T, W), lambda b, c: (b, c, 0))
    return pl.pallas_call(
        functools.partial(_mlstm_kernel, L=L),
        grid=(B, S // T),
        in_specs=[kspec, kspec, vspec,
                  pl.BlockSpec((1, GATE_ROWS, T), lambda b, c: (b, 0, c)),
                  vspec, vspec, pl.BlockSpec((1, W), lambda b, c: (0, 0))],
        out_specs=vspec,
        out_shape=jax.ShapeDtypeStruct((B, S, W), BF16),
        scratch_shapes=[pltpu.VMEM((HEADS, HEAD_DK, ML_DV + HEAD_DK), F32),
                        pltpu.VMEM((GATE_ROWS, 128), F32)],
        compiler_params=_cparams(("parallel", "arbitrary")),
        name="mlstm",
    )(q, k, v, gt, og, z, g)


def _gate_rows(w_cols):
    wt = w_cols.T
    return jnp.pad(wt, ((0, GATE_ROWS - wt.shape[0]), (0, 0))).astype(BF16)


def kernel(x, hgrn_lb_logits, ab_w_in, ab_fox_bf, ab_hgrn_norm_g, ab_w_out, c_w_in,
           c_conv_w, c_conv_b, c_bi, c_bf, c_norm_g, c_w_out, ln_g, ln_b):
    B, S, D = x.shape
    W = HEADS * HEAD_DK
    TM = min(1024, S)

    lb_table = jnp.cumsum(jax.nn.softmax(hgrn_lb_logits.astype(F32), axis=0), axis=0)
    h = x.astype(F32)

    w_in = ab_w_in[0].astype(F32)
    w_main = jnp.concatenate([w_in[:, :7 * W], w_in[:, 7 * W + HEADS:]], axis=1).astype(BF16)
    w_gate = _gate_rows(w_in[:, 7 * W:7 * W + HEADS])
    bfb = jnp.broadcast_to(jnp.pad(ab_fox_bf[0].astype(F32), (0, GATE_ROWS - HEADS))[:, None], (GATE_ROWS, TM))
    hq, lf, hk, hv, hz, fq, fk, fv, fz, c = _inproj_ab(
        h, w_main, w_gate, lb_table[0][None, :], bfb, tm=TM)
    hg = _hgrn2(hq, lf, hk, hv, hz, ab_hgrn_norm_g[0].astype(F32)[None, :], T=min(256, S))
    fx = _fox(fq, fk, fv, c, fz, tq=min(2048, S), tk=min(256, S))
    h = _outproj_ln([hg.reshape(B * S, W), fx.reshape(B * S, W)], ab_w_out[0].astype(BF16),
                    h.reshape(B * S, D), ln_g[0].astype(F32)[None, :], ln_b[0].astype(F32)[None, :],
                    tm=TM).reshape(B, S, D)

    w_in = c_w_in[0].astype(F32)
    KW, VW = W, HEADS * ML_DV
    o_i = 2 * KW + VW
    w_main = jnp.concatenate([w_in[:, :o_i], w_in[:, o_i + 2 * HEADS:]], axis=1).astype(BF16)
    w_gate = _gate_rows(w_in[:, o_i:o_i + 2 * HEADS])
    gb = jnp.broadcast_to(jnp.concatenate([c_bi[0], c_bf[0]]).astype(F32)[:, None], (GATE_ROWS, TM))
    mq, mk, mv, og, mz, gt = _inproj_c(h, w_main, w_gate, c_conv_w[0].astype(F32),
                                       c_conv_b[0].astype(F32)[None, :], gb, tm=TM)
    ht = _mlstm(mq, mk, mv, gt, og, mz, c_norm_g[0].astype(F32)[None, :], L=ML_CHUNK, T=min(2 * ML_CHUNK, S))
    h = _outproj_ln([ht.reshape(B * S, VW)], c_w_out[0].astype(BF16), h.reshape(B * S, D),
                    ln_g[1].astype(F32)[None, :], ln_b[1].astype(F32)[None, :], tm=TM).reshape(B, S, D)
    return h.astype(x.dtype)
```

```python
import functools
import math

import jax
import jax.numpy as jnp
from jax import lax
from jax.experimental import pallas as pl
from jax.experimental.pallas import tpu as pltpu

F32 = jnp.float32
BF16 = jnp.bfloat16

DEPTH = 2
ALPHA = (2 * DEPTH) ** 0.25
LN_EPS = 1e-5
RMS_EPS = 1e-6

HEADS = 4
HEAD_DK = 128
ML_DV = 256
HG_CHUNK = 32
ML_CONV = 4
ML_CHUNK = 256
GATE_ROWS = 8

VMEM_LIMIT = 56 * 1024 * 1024
LOG2E = math.log2(math.e)


def _cparams(sem):
    return pltpu.CompilerParams(dimension_semantics=sem, vmem_limit_bytes=VMEM_LIMIT)


def _sigmoid(z):
    return 0.5 + 0.5 * jnp.tanh(0.5 * z)


def _silu(z):
    hz = 0.5 * z
    return hz + hz * jnp.tanh(hz)


def _log_sigmoid(z):
    return jnp.minimum(z, 0.0) - jnp.log(1.0 + jnp.exp(-jnp.abs(z)))


def _cumsum_lanes(x, seg=None):
    seg = seg or x.shape[-1]
    pos = lax.broadcasted_iota(jnp.int32, x.shape, x.ndim - 1) % seg
    sh = 1
    while sh < seg:
        x = x + jnp.where(pos >= sh, pltpu.roll(x, sh, x.ndim - 1), 0.0)
        sh *= 2
    return x


def _inproj_ab_kernel(x_ref, w_ref, wg_ref, lb_ref, bf_ref, g_ref,
                      hg_ref, fq_ref, fk_ref, fv_ref, fz_ref, c_ref,
                      carry_ref, st_ref, xb_ref, q_ref, lf_ref, k_ref, v_ref, z_ref,
                      *, width, fq_scale, T, piece):
    si = pl.program_id(1)
    xb_ref[...] = x_ref[0].astype(BF16)
    xb = xb_ref[...]

    g = lax.dot_general(wg_ref[...], xb, (((1,), (1,)), ((), ())), preferred_element_type=F32)
    ls = _log_sigmoid(g + bf_ref[...])

    @pl.when(si == 0)
    def _():
        carry_ref[...] = jnp.zeros_like(carry_ref)
        st_ref[...] = jnp.zeros_like(st_ref)

    c = _cumsum_lanes(ls) + carry_ref[:, :1]
    c_ref[0] = c
    carry_ref[...] = jnp.broadcast_to(c[:, -1:], carry_ref.shape)

    def mm(lo, n):
        return jnp.dot(xb_ref[...], w_ref[:, lo:lo + n], preferred_element_type=F32)

    q_ref[0] = mm(0, width).astype(BF16)
    lb = lb_ref[...]
    f = lb + (1.0 - lb) * _sigmoid(mm(width, width))
    lf_ref[0] = jnp.log(f)
    k_ref[0] = (1.0 - f).astype(BF16)
    v_ref[0] = mm(2 * width, width).astype(BF16)
    z_ref[0] = _silu(mm(3 * width, width)).astype(BF16)

    fox_out = (fq_ref, fk_ref, fv_ref, fz_ref)
    per_group = width // piece

    def fox_piece(n):
        grp, part = n // per_group, n % per_group
        cs = slice(part * piece, (part + 1) * piece)
        y = mm((4 + grp) * width + part * piece, piece)
        if grp == 0:
            y = y * fq_scale
        elif grp == 3:
            y = _silu(y)
        fox_out[grp][0, :, cs] = y.astype(BF16)

    n_sub = x_ref.shape[1] // T
    for sub in range(n_sub):
        _hgrn2_tile(slice(sub * T, (sub + 1) * T), q_ref, lf_ref, k_ref, v_ref, z_ref, g_ref, hg_ref, st_ref,
                    T=T, after_head=lambda h, sub=sub: fox_piece(sub * HEADS + h))


def _inproj_ab(x, w, wg, lb, bfb, g, *, tm, T):
    B, S, D = x.shape
    width = w.shape[1] // 8
    piece = 4 * width // ((tm // T) * HEADS)
    assert piece % 128 == 0 and width % piece == 0
    wide = jax.ShapeDtypeStruct((B, S, width), BF16)
    tok = pl.BlockSpec((1, tm, width), lambda b, s: (b, s, 0))
    const = lambda shape: pl.BlockSpec(shape, lambda b, s: (0,) * len(shape))
    return pl.pallas_call(
        functools.partial(_inproj_ab_kernel, width=width, fq_scale=HEAD_DK ** -0.5 * LOG2E, T=T, piece=piece),
        grid=(B, S // tm),
        in_specs=[pl.BlockSpec((1, tm, D), lambda b, s: (b, s, 0)),
                  const(w.shape), const(wg.shape), const(lb.shape), const(bfb.shape), const(g.shape)],
        out_specs=[tok, tok, tok, tok, tok,
                   pl.BlockSpec((1, GATE_ROWS, tm), lambda b, s: (b, 0, s))],
        out_shape=[wide, wide, wide, wide, wide,
                   jax.ShapeDtypeStruct((B, GATE_ROWS, S), F32)],
        scratch_shapes=[pltpu.VMEM((GATE_ROWS, 128), F32), pltpu.VMEM((HEADS, HEAD_DK, HEAD_DK), F32),
                        pltpu.VMEM((tm, D), BF16),
                        pltpu.VMEM((1, tm, width), BF16), pltpu.VMEM((1, tm, width), F32),
                        pltpu.VMEM((1, tm, width), BF16), pltpu.VMEM((1, tm, width), BF16),
                        pltpu.VMEM((1, tm, width), BF16)],
        compiler_params=_cparams(("parallel", "arbitrary")),
        name="inproj_ab_hgrn2",
    )(x, w, wg, lb, bfb, g)


def _hgrn2_tile(rs, q_ref, lf_ref, k_ref, v_ref, z_ref, g_ref, o_ref, st_ref, *, T, after_head=None):
    L = HG_CHUNK
    BL = 2 * L
    row = lax.broadcasted_iota(jnp.int32, (T, T), 0)
    col = lax.broadcasted_iota(jnp.int32, (T, T), 1)
    rc, cc = row // L, col // L
    same_chunk = rc == cc
    same_chunk_causal = same_chunk & (col <= row)
    prev_chunk_in_block = (rc == cc + 1) & (rc % 2 == 1)
    tri = jnp.where(same_chunk_causal, 1.0, 0.0).astype(BF16)
    last = jnp.where(same_chunk, 1.0, 0.0).astype(BF16)
    odd_row = (lax.broadcasted_iota(jnp.int32, (T, HEAD_DK), 0) // L) % 2 == 1

    lf_all = lf_ref[0, rs, :]
    lf_hi = lf_all.astype(BF16)
    lf_lo = (lf_all - lf_hi.astype(F32)).astype(BF16)
    b_all = (jnp.dot(tri, lf_hi, preferred_element_type=F32)
             + jnp.dot(tri, lf_lo, preferred_element_type=F32))
    btot_all = (jnp.dot(last, lf_hi, preferred_element_type=F32)
                + jnp.dot(last, lf_lo, preferred_element_type=F32))

    for h in range(HEADS):
        sl = slice(h * HEAD_DK, (h + 1) * HEAD_DK)
        b = b_all[:, sl]
        btot = btot_all[:, sl]
        q = q_ref[0, rs, sl].astype(F32)
        k = k_ref[0, rs, sl].astype(F32)
        v = v_ref[0, rs, sl]
        dec_tot = jnp.exp(btot)
        dec_prev = pltpu.roll(dec_tot, L, 0)
        dec_next = pltpu.roll(dec_tot, T - L, 0)
        q_dec32 = q * jnp.exp(b)
        k_end32 = k * jnp.exp(btot - b)
        q_dec = q_dec32.astype(BF16)
        k_inv = (k * jnp.exp(-b)).astype(BF16)
        k_end = k_end32.astype(BF16)
        q_blk = jnp.where(odd_row, q_dec32 * dec_prev, q_dec32).astype(BF16)
        k_blk = jnp.where(odd_row, k_end32, k_end32 * dec_next).astype(BF16)
        dec_blk = dec_tot * dec_next

        nt = (((1,), (1,)), ((), ()))
        a_in = lax.dot_general(q_dec, k_inv, nt, preferred_element_type=F32)
        a_x = lax.dot_general(q_dec, k_end, nt, preferred_element_type=F32)
        a = jnp.where(same_chunk_causal, a_in, jnp.where(prev_chunk_in_block, a_x, 0.0)).astype(BF16)
        o = jnp.dot(a, v, preferred_element_type=F32)

        st = st_ref[h]
        inter = []
        for c in range(T // BL):
            bs = slice(c * BL, (c + 1) * BL)
            inter.append(lax.dot_general(q_blk[bs], st.astype(BF16), nt, preferred_element_type=F32))
            upd = lax.dot_general(v[bs], k_blk[bs], (((0,), (0,)), ((), ())),
                                  preferred_element_type=F32)
            st = st * dec_blk[c * BL:c * BL + 1, :] + upd
        st_ref[h] = st
        o = o + jnp.concatenate(inter, axis=0)

        ms = jnp.mean(o * o, axis=-1, keepdims=True)
        o = o * lax.rsqrt(ms + RMS_EPS) * g_ref[:, sl]
        o_ref[0, rs, sl] = (o * z_ref[0, rs, sl].astype(F32)).astype(BF16)
        if after_head is not None:
            after_head(h)


def _fox_kernel(q_ref, k_ref, v_ref, c_ref, z_ref, o_ref, vaug_ref, m_ref, acc_ref, *, tq, tk):
    h = pl.program_id(1)
    qi = pl.program_id(2)

    @pl.when(qi == 0)
    def _():
        vaug_ref[:, :HEAD_DK] = v_ref[0]
        vaug_ref[:, HEAD_DK:] = jnp.ones((vaug_ref.shape[0], HEAD_DK), BF16)

    per_q = tq // tk

    def c_row(j):
        return c_ref[0, j, pl.ds(h, 1), :] * LOG2E

    cref = c_row(qi * per_q)[:, :1]
    m_ref[...] = jnp.full_like(m_ref, -jnp.inf)
    acc_ref[...] = jnp.zeros_like(acc_ref)

    def chunk(j, r0, masked):
        ks = pl.multiple_of(j * tk, tk)
        s = lax.dot_general(q_ref[0, r0:, :], k_ref[0, pl.ds(ks, tk), :], (((1,), (1,)), ((), ())),
                            preferred_element_type=F32)
        s = s + (cref - c_row(j))
        if masked:
            row = lax.broadcasted_iota(jnp.int32, s.shape, 0)
            col = lax.broadcasted_iota(jnp.int32, s.shape, 1)
            s = jnp.where(col <= row, s, -jnp.inf)
        m_old = m_ref[r0:, :]
        m_new = jnp.maximum(m_old, jnp.max(s, axis=-1, keepdims=True))
        alpha = jnp.exp2(m_old - m_new)
        p = jnp.exp2(s - jnp.tile(m_new, (1, tk // 128))).astype(BF16)
        acc_ref[r0:, :] = (jnp.tile(alpha, (1, 2)) * acc_ref[r0:, :]
                           + jnp.dot(p, vaug_ref[pl.ds(ks, tk), :], preferred_element_type=F32))
        m_ref[r0:, :] = m_new

    def body(i, carry):
        for u in range(per_q):
            chunk(i * per_q + u, 0, False)
        return carry

    lax.fori_loop(0, qi, body, 0)
    for u in range(per_q):
        chunk(qi * per_q + u, u * tk, True)
    acc = acc_ref[...]
    o = acc[:, :HEAD_DK] / acc[:, HEAD_DK:]
    o_ref[0] = (o * z_ref[0].astype(F32)).astype(BF16)


def _fox(q, k, v, c, z, *, tq, tk):
    B, S, W = q.shape
    nq, nk = S // tq, S // tk
    c4 = c.reshape(B, GATE_ROWS, nk, tk).transpose(0, 2, 1, 3)
    qspec = pl.BlockSpec((1, tq, HEAD_DK), lambda b, h, i: (b, i, h))
    kspec = pl.BlockSpec((1, S, HEAD_DK), lambda b, h, i: (b, 0, h))
    return pl.pallas_call(
        functools.partial(_fox_kernel, tq=tq, tk=tk),
        grid=(B, HEADS, nq),
        in_specs=[qspec, kspec, kspec,
                  pl.BlockSpec((1, nk, GATE_ROWS, tk), lambda b, h, i: (b, 0, 0, 0)),
                  qspec],
        out_specs=qspec,
        out_shape=jax.ShapeDtypeStruct((B, S, W), BF16),
        scratch_shapes=[pltpu.VMEM((S, 2 * HEAD_DK), BF16), pltpu.VMEM((tq, 128), F32),
                        pltpu.VMEM((tq, 2 * HEAD_DK), F32)],
        compiler_params=_cparams(("parallel", "parallel", "arbitrary")),
        name="fox_attention",
    )(q, k, v, c4, z)


def _outproj_ln_kernel(*refs, n_act):
    act_refs = refs[:n_act]
    w_ref, x_ref, g_ref, b_ref, o_ref = refs[n_act:]
    y = ALPHA * x_ref[...]
    off = 0
    for a_ref in act_refs:
        wd = a_ref.shape[1]
        y = y + jnp.dot(a_ref[...], w_ref[off:off + wd, :], preferred_element_type=F32)
        off += wd
    mu = jnp.mean(y, axis=-1, keepdims=True)
    yc = y - mu
    var = jnp.mean(yc * yc, axis=-1, keepdims=True)
    o_ref[...] = yc * lax.rsqrt(var + LN_EPS) * g_ref[...] + b_ref[...]


def _outproj_ln(acts, w, x, g, b, *, tm):
    N, D = x.shape
    tok = lambda wd: pl.BlockSpec((tm, wd), lambda i: (i, 0))
    const = lambda shape: pl.BlockSpec(shape, lambda i: (0,) * len(shape))
    return pl.pallas_call(
        functools.partial(_outproj_ln_kernel, n_act=len(acts)),
        grid=(N // tm,),
        in_specs=[tok(a.shape[1]) for a in acts] + [const(w.shape), tok(D), const(g.shape), const(b.shape)],
        out_specs=tok(D),
        out_shape=jax.ShapeDtypeStruct((N, D), F32),
        compiler_params=_cparams(("parallel",)),
        name="outproj_ln",
    )(*acts, w, x, g, b)


def _inproj_c_kernel(x_ref, w_ref, wg_ref, cw_ref, cb_ref, gb_ref,
                     q_ref, k_ref, v_ref, og_ref, z_ref, gt_ref,
                     ubuf_ref, *, tm, kw, vw, k_scale):
    si = pl.program_id(1)
    xb = x_ref[0].astype(BF16)

    g = lax.dot_general(wg_ref[...], xb, (((1,), (1,)), ((), ())), preferred_element_type=F32) + gb_ref[...]
    rowi = lax.broadcasted_iota(jnp.int32, g.shape, 0)
    b = _cumsum_lanes(_log_sigmoid(g), seg=ML_CHUNK)
    gt_ref[0] = jnp.where(rowi < HEADS, g - pltpu.roll(b, HEADS, 0), b)

    def mm(lo, hi):
        return jnp.dot(xb, w_ref[:, lo:hi], preferred_element_type=F32)

    pad = 8

    @pl.when(si == 0)
    def _():
        ubuf_ref[0:pad, :] = jnp.zeros((pad, 2 * kw), F32)

    cwid = 256
    for c0 in range(0, 2 * kw, cwid):
        cs = slice(c0, c0 + cwid)
        ubuf_ref[pad:pad + tm, cs] = mm(c0, c0 + cwid)
        acc = cb_ref[:, cs]
        for j in range(ML_CONV):
            st = pad - (ML_CONV - 1) + j
            acc = acc + cw_ref[j:j + 1, cs] * ubuf_ref[st:st + tm, cs]
        ubuf_ref[0:pad, cs] = ubuf_ref[tm:tm + pad, cs]
        if c0 < kw:
            q_ref[0, :, cs] = _silu(acc).astype(BF16)
        else:
            k_ref[0, :, c0 - kw:c0 - kw + cwid] = (_silu(acc) * k_scale).astype(BF16)
    o0 = 2 * kw
    for c0 in range(0, vw, cwid):
        cs = slice(c0, c0 + cwid)
        v_ref[0, :, cs] = mm(o0 + c0, o0 + c0 + cwid).astype(BF16)
        og_ref[0, :, cs] = _sigmoid(mm(o0 + vw + c0, o0 + vw + c0 + cwid)).astype(BF16)
        z_ref[0, :, cs] = _silu(mm(o0 + 2 * vw + c0, o0 + 2 * vw + c0 + cwid)).astype(BF16)


def _inproj_c(x, w, wg, cw, cb, gb, *, tm):
    B, S, D = x.shape
    kw = cw.shape[1] // 2
    vw = (w.shape[1] - 2 * kw) // 3
    const = lambda shape: pl.BlockSpec(shape, lambda b, s: (0,) * len(shape))
    tok = lambda wd: pl.BlockSpec((1, tm, wd), lambda b, s: (b, s, 0))
    return pl.pallas_call(
        functools.partial(_inproj_c_kernel, tm=tm, kw=kw, vw=vw, k_scale=HEAD_DK ** -0.5),
        grid=(B, S // tm),
        in_specs=[pl.BlockSpec((1, tm, D), lambda b, s: (b, s, 0)),
                  const(w.shape), const(wg.shape), const(cw.shape), const(cb.shape), const(gb.shape)],
        out_specs=[tok(kw), tok(kw), tok(vw), tok(vw), tok(vw),
                   pl.BlockSpec((1, GATE_ROWS, tm), lambda b, s: (b, 0, s))],
        out_shape=[jax.ShapeDtypeStruct((B, S, kw), BF16), jax.ShapeDtypeStruct((B, S, kw), BF16),
                   jax.ShapeDtypeStruct((B, S, vw), BF16), jax.ShapeDtypeStruct((B, S, vw), BF16),
                   jax.ShapeDtypeStruct((B, S, vw), BF16), jax.ShapeDtypeStruct((B, GATE_ROWS, S), F32)],
        scratch_shapes=[pltpu.VMEM((tm + 8, 2 * kw), F32)],
        compiler_params=_cparams(("parallel", "arbitrary")),
        name="inproj_c",
    )(x, w, wg, cw, cb, gb)


def _mlstm_kernel(q_ref, k_ref, v_ref, gt_ref, og_ref, z_ref, g_ref, o_ref, c_ref, m_ref, *, L):
    @pl.when(pl.program_id(1) == 0)
    def _():
        c_ref[...] = jnp.zeros_like(c_ref)
        m_ref[...] = jnp.zeros_like(m_ref)

    for sub in range(q_ref.shape[1] // L):
        for h in range(HEADS):
            _mlstm_head(h, slice(sub * L, (sub + 1) * L),
                        q_ref, k_ref, v_ref, gt_ref, og_ref, z_ref, g_ref, o_ref, c_ref, m_ref, L=L)


def _mlstm_head(h, rs, q_ref, k_ref, v_ref, gt_ref, og_ref, z_ref, g_ref, o_ref, c_ref, m_ref, *, L):
    row = lax.broadcasted_iota(jnp.int32, (L, L), 0)
    col = lax.broadcasted_iota(jnp.int32, (L, L), 1)
    tril = col <= row
    eye = col == row
    ks = slice(h * HEAD_DK, (h + 1) * HEAD_DK)
    vs = slice(h * ML_DV, (h + 1) * ML_DV)
    q = q_ref[0, rs, ks]
    k = k_ref[0, rs, ks]
    v = v_ref[0, rs, vs]
    a_row = gt_ref[0, h:h + 1, rs]
    b_row = gt_ref[0, HEADS + h:HEADS + h + 1, rs]
    m_prev = m_ref[h:h + 1, :1]

    M_col = jnp.maximum(m_prev, jnp.max(jnp.where(tril, a_row, -jnp.inf), axis=-1, keepdims=True))
    b_col = jnp.sum(jnp.where(eye, b_row, 0.0), axis=-1, keepdims=True)
    w = jnp.where(tril, jnp.exp(a_row - M_col), 0.0)
    s_inter = jnp.exp(m_prev - M_col)

    v_ext = jnp.concatenate([v, jnp.ones((L, HEAD_DK), BF16)], axis=1)
    qk = lax.dot_general(q, k, (((1,), (1,)), ((), ())), preferred_element_type=F32) * w
    ne = (jnp.dot(qk.astype(BF16), v_ext, preferred_element_type=F32)
          + s_inter * jnp.dot(q, c_ref[h].astype(BF16), preferred_element_type=F32))
    den = jnp.maximum(jnp.abs(ne[:, ML_DV:]), jnp.exp(-(b_col + M_col)))
    hval = ne[:, :ML_DV] / jnp.tile(den, (1, ML_DV // HEAD_DK))

    M_last = jnp.maximum(m_prev, jnp.max(a_row, axis=-1, keepdims=True))
    wk_row = jnp.exp(a_row - M_last)
    decay = jnp.exp(m_prev - M_last)
    wk_col = jnp.sum(jnp.where(eye, wk_row, 0.0), axis=-1, keepdims=True)
    kw = (k.astype(F32) * wk_col).astype(BF16)
    c_ref[h] = decay * c_ref[h] + lax.dot_general(kw, v_ext, (((0,), (0,)), ((), ())),
                                                  preferred_element_type=F32)
    m_ref[h:h + 1, :] = jnp.broadcast_to(b_row[:, L - 1:] + M_last, (1, m_ref.shape[1]))

    ht = og_ref[0, rs, vs].astype(F32) * hval
    ms = jnp.mean(ht * ht, axis=-1, keepdims=True)
    ht = ht * lax.rsqrt(ms + RMS_EPS) * g_ref[:, vs]
    o_ref[0, rs, vs] = (ht * z_ref[0, rs, vs].astype(F32)).astype(BF16)


def _mlstm(q, k, v, gt, og, z, g, *, L, T):
    B, S, KW = q.shape
    W = v.shape[2]
    kspec = pl.BlockSpec((1, T, KW), lambda b, c: (b, c, 0))
    vspec = pl.BlockSpec((1, T, W), lambda b, c: (b, c, 0))
    return pl.pallas_call(
        functools.partial(_mlstm_kernel, L=L),
        grid=(B, S // T),
        in_specs=[kspec, kspec, vspec,
                  pl.BlockSpec((1, GATE_ROWS, T), lambda b, c: (b, 0, c)),
                  vspec, vspec, pl.BlockSpec((1, W), lambda b, c: (0, 0))],
        out_specs=vspec,
        out_shape=jax.ShapeDtypeStruct((B, S, W), BF16),
        scratch_shapes=[pltpu.VMEM((HEADS, HEAD_DK, ML_DV + HEAD_DK), F32),
                        pltpu.VMEM((GATE_ROWS, 128), F32)],
        compiler_params=_cparams(("parallel", "arbitrary")),
        name="mlstm",
    )(q, k, v, gt, og, z, g)


def _gate_rows(w_cols):
    wt = w_cols.T
    return jnp.pad(wt, ((0, GATE_ROWS - wt.shape[0]), (0, 0))).astype(BF16)


def kernel(x, hgrn_lb_logits, ab_w_in, ab_fox_bf, ab_hgrn_norm_g, ab_w_out, c_w_in,
           c_conv_w, c_conv_b, c_bi, c_bf, c_norm_g, c_w_out, ln_g, ln_b):
    B, S, D = x.shape
    W = HEADS * HEAD_DK
    TM = min(1024, S)
    TM_AB = min(512, S)

    lb_table = jnp.cumsum(jax.nn.softmax(hgrn_lb_logits.astype(F32), axis=0), axis=0)
    h = x.astype(F32)

    w_in = ab_w_in[0].astype(F32)
    w_main = jnp.concatenate([w_in[:, :7 * W], w_in[:, 7 * W + HEADS:]], axis=1).astype(BF16)
    w_gate = _gate_rows(w_in[:, 7 * W:7 * W + HEADS])
    bfb = jnp.broadcast_to(jnp.pad(ab_fox_bf[0].astype(F32), (0, GATE_ROWS - HEADS))[:, None], (GATE_ROWS, TM_AB))
    hg, fq, fk, fv, fz, c = _inproj_ab(h, w_main, w_gate, lb_table[0][None, :], bfb,
                                       ab_hgrn_norm_g[0].astype(F32)[None, :], tm=TM_AB, T=min(256, S))
    fx = _fox(fq, fk, fv, c, fz, tq=min(2048, S), tk=min(256, S))
    h = _outproj_ln([hg.reshape(B * S, W), fx.reshape(B * S, W)], ab_w_out[0].astype(BF16),
                    h.reshape(B * S, D), ln_g[0].astype(F32)[None, :], ln_b[0].astype(F32)[None, :],
                    tm=TM).reshape(B, S, D)

    w_in = c_w_in[0].astype(F32)
    KW, VW = W, HEADS * ML_DV
    o_i = 2 * KW + VW
    w_main = jnp.concatenate([w_in[:, :o_i], w_in[:, o_i + 2 * HEADS:]], axis=1).astype(BF16)
    w_gate = _gate_rows(w_in[:, o_i:o_i + 2 * HEADS])
    gb = jnp.broadcast_to(jnp.concatenate([c_bi[0], c_bf[0]]).astype(F32)[:, None], (GATE_ROWS, TM))
    mq, mk, mv, og, mz, gt = _inproj_c(h, w_main, w_gate, c_conv_w[0].astype(F32),
                                       c_conv_b[0].astype(F32)[None, :], gb, tm=TM)
    ht = _mlstm(mq, mk, mv, gt, og, mz, c_norm_g[0].astype(F32)[None, :], L=ML_CHUNK, T=min(4 * ML_CHUNK, S))
    h = _outproj_ln([ht.reshape(B * S, VW)], c_w_out[0].astype(BF16), h.reshape(B * S, D),
                    ln_g[1].astype(F32)[None, :], ln_b[1].astype(F32)[None, :], tm=TM).reshape(B, S, D)
    return h.astype(x.dtype)
```

```python
import functools
import math

import jax
import jax.numpy as jnp
from jax import lax
from jax.experimental import pallas as pl
from jax.experimental.pallas import tpu as pltpu

F32 = jnp.float32
BF16 = jnp.bfloat16

DEPTH = 2
ALPHA = (2 * DEPTH) ** 0.25
LN_EPS = 1e-5
RMS_EPS = 1e-6

HEADS = 4
HEAD_DK = 128
ML_DV = 256
HG_CHUNK = 32
ML_CONV = 4
ML_CHUNK = 256
GATE_ROWS = 8

VMEM_LIMIT = 56 * 1024 * 1024
LOG2E = math.log2(math.e)


def _cparams(sem):
    return pltpu.CompilerParams(dimension_semantics=sem, vmem_limit_bytes=VMEM_LIMIT)


def _sigmoid(z):
    return 0.5 + 0.5 * jnp.tanh(0.5 * z)


def _silu(z):
    hz = 0.5 * z
    return hz + hz * jnp.tanh(hz)


def _log_sigmoid(z):
    return jnp.minimum(z, 0.0) - jnp.log(1.0 + jnp.exp(-jnp.abs(z)))


def _cumsum_lanes(x, seg=None):
    seg = seg or x.shape[-1]
    pos = lax.broadcasted_iota(jnp.int32, x.shape, x.ndim - 1) % seg
    sh = 1
    while sh < seg:
        x = x + jnp.where(pos >= sh, pltpu.roll(x, sh, x.ndim - 1), 0.0)
        sh *= 2
    return x


def _mm_split(x, w_ref, wb_ref, lo, n):
    na = w_ref.shape[1]
    if lo + n <= na:
        return jnp.dot(x, w_ref[:, lo:lo + n], preferred_element_type=F32)
    assert lo >= na
    return jnp.dot(x, wb_ref[:, lo - na:lo - na + n], preferred_element_type=F32)


def _inproj_ab_kernel(x_ref, w_ref, wb_ref, wg_ref, lb_ref, bf_ref, g_ref,
                      hg_ref, fq_ref, fk_ref, fv_ref, fz_ref, c_ref,
                      carry_ref, st_ref, xb_ref, q_ref, lf_ref, k_ref, v_ref, z_ref,
                      *, width, fq_scale, T, piece):
    si = pl.program_id(1)
    xb_ref[...] = x_ref[0].astype(BF16)
    xb = xb_ref[...]

    g = lax.dot_general(wg_ref[...], xb, (((1,), (1,)), ((), ())), preferred_element_type=F32)
    ls = _log_sigmoid(g + bf_ref[...])

    @pl.when(si == 0)
    def _():
        carry_ref[...] = jnp.zeros_like(carry_ref)
        st_ref[...] = jnp.zeros_like(st_ref)

    c = _cumsum_lanes(ls) + carry_ref[:, :1]
    ck = c_ref.shape[3]
    for j in range(c_ref.shape[1]):
        c_ref[0, j] = c[:, j * ck:(j + 1) * ck]
    carry_ref[...] = jnp.broadcast_to(c[:, -1:], carry_ref.shape)

    def mm(lo, n):
        return _mm_split(xb_ref[...], w_ref, wb_ref, lo, n)

    q_ref[0] = mm(0, width).astype(BF16)
    lb = lb_ref[...]
    f = lb + (1.0 - lb) * _sigmoid(mm(width, width))
    lf_ref[0] = jnp.log(f)
    k_ref[0] = (1.0 - f).astype(BF16)
    v_ref[0] = mm(2 * width, width).astype(BF16)
    z_ref[0] = _silu(mm(3 * width, width)).astype(BF16)

    fox_out = (fq_ref, fk_ref, fv_ref, fz_ref)
    per_group = width // piece

    def fox_piece(n):
        grp, part = n // per_group, n % per_group
        cs = slice(part * piece, (part + 1) * piece)
        y = mm((4 + grp) * width + part * piece, piece)
        if grp == 0:
            y = y * fq_scale
        elif grp == 3:
            y = _silu(y)
        fox_out[grp][0, :, cs] = y.astype(BF16)

    n_sub = x_ref.shape[1] // T
    for sub in range(n_sub):
        _hgrn2_tile(slice(sub * T, (sub + 1) * T), q_ref, lf_ref, k_ref, v_ref, z_ref, g_ref, hg_ref, st_ref,
                    T=T, after_head=lambda h, sub=sub: fox_piece(sub * HEADS + h))


def _inproj_ab(x, w, wb, wg, lb, bfb, g, *, tm, T, ck):
    B, S, D = x.shape
    width = (w.shape[1] + wb.shape[1]) // 8
    piece = 4 * width // ((tm // T) * HEADS)
    assert piece % 128 == 0 and width % piece == 0
    wide = jax.ShapeDtypeStruct((B, S, width), BF16)
    tok = pl.BlockSpec((1, tm, width), lambda b, s: (b, s, 0))
    const = lambda shape: pl.BlockSpec(shape, lambda b, s: (0,) * len(shape))
    return pl.pallas_call(
        functools.partial(_inproj_ab_kernel, width=width, fq_scale=HEAD_DK ** -0.5 * LOG2E, T=T, piece=piece),
        grid=(B, S // tm),
        in_specs=[pl.BlockSpec((1, tm, D), lambda b, s: (b, s, 0)),
                  const(w.shape), const(wb.shape), const(wg.shape), const(lb.shape), const(bfb.shape),
                  const(g.shape)],
        out_specs=[tok, tok, tok, tok, tok,
                   pl.BlockSpec((1, tm // ck, GATE_ROWS, ck), lambda b, s: (b, s, 0, 0))],
        out_shape=[wide, wide, wide, wide, wide,
                   jax.ShapeDtypeStruct((B, S // ck, GATE_ROWS, ck), F32)],
        scratch_shapes=[pltpu.VMEM((GATE_ROWS, 128), F32), pltpu.VMEM((HEADS, HEAD_DK, HEAD_DK), F32),
                        pltpu.VMEM((tm, D), BF16),
                        pltpu.VMEM((1, tm, width), BF16), pltpu.VMEM((1, tm, width), F32),
                        pltpu.VMEM((1, tm, width), BF16), pltpu.VMEM((1, tm, width), BF16),
                        pltpu.VMEM((1, tm, width), BF16)],
        compiler_params=_cparams(("parallel", "arbitrary")),
        name="inproj_ab_hgrn2",
    )(x, w, wb, wg, lb, bfb, g)


def _hgrn2_tile(rs, q_ref, lf_ref, k_ref, v_ref, z_ref, g_ref, o_ref, st_ref, *, T, after_head=None):
    L = HG_CHUNK
    BL = 2 * L
    row = lax.broadcasted_iota(jnp.int32, (T, T), 0)
    col = lax.broadcasted_iota(jnp.int32, (T, T), 1)
    rc, cc = row // L, col // L
    same_chunk = rc == cc
    same_chunk_causal = same_chunk & (col <= row)
    prev_chunk_in_block = (rc == cc + 1) & (rc % 2 == 1)
    tri = jnp.where(same_chunk_causal, 1.0, 0.0).astype(BF16)
    last = jnp.where(same_chunk, 1.0, 0.0).astype(BF16)
    odd_row = (lax.broadcasted_iota(jnp.int32, (T, HEAD_DK), 0) // L) % 2 == 1

    lf_all = lf_ref[0, rs, :]
    lf_hi = lf_all.astype(BF16)
    lf_lo = (lf_all - lf_hi.astype(F32)).astype(BF16)
    b_all = (jnp.dot(tri, lf_hi, preferred_element_type=F32)
             + jnp.dot(tri, lf_lo, preferred_element_type=F32))
    btot_all = (jnp.dot(last, lf_hi, preferred_element_type=F32)
                + jnp.dot(last, lf_lo, preferred_element_type=F32))

    for h in range(HEADS):
        sl = slice(h * HEAD_DK, (h + 1) * HEAD_DK)
        b = b_all[:, sl]
        btot = btot_all[:, sl]
        q = q_ref[0, rs, sl].astype(F32)
        k = k_ref[0, rs, sl].astype(F32)
        v = v_ref[0, rs, sl]
        dec_tot = jnp.exp(btot)
        dec_prev = pltpu.roll(dec_tot, L, 0)
        dec_next = pltpu.roll(dec_tot, T - L, 0)
        q_dec32 = q * jnp.exp(b)
        k_end32 = k * jnp.exp(btot - b)
        q_dec = q_dec32.astype(BF16)
        k_inv = (k * jnp.exp(-b)).astype(BF16)
        k_end = k_end32.astype(BF16)
        q_blk = jnp.where(odd_row, q_dec32 * dec_prev, q_dec32).astype(BF16)
        k_blk = jnp.where(odd_row, k_end32, k_end32 * dec_next).astype(BF16)
        dec_blk = dec_tot * dec_next

        nt = (((1,), (1,)), ((), ()))
        a_in = lax.dot_general(q_dec, k_inv, nt, preferred_element_type=F32)
        a_x = lax.dot_general(q_dec, k_end, nt, preferred_element_type=F32)
        a = jnp.where(same_chunk_causal, a_in, jnp.where(prev_chunk_in_block, a_x, 0.0)).astype(BF16)
        o = jnp.dot(a, v, preferred_element_type=F32)

        st = st_ref[h]
        inter = []
        for c in range(T // BL):
            bs = slice(c * BL, (c + 1) * BL)
            inter.append(lax.dot_general(q_blk[bs], st.astype(BF16), nt, preferred_element_type=F32))
            upd = lax.dot_general(v[bs], k_blk[bs], (((0,), (0,)), ((), ())),
                                  preferred_element_type=F32)
            st = st * dec_blk[c * BL:c * BL + 1, :] + upd
        st_ref[h] = st
        o = o + jnp.concatenate(inter, axis=0)

        ms = jnp.mean(o * o, axis=-1, keepdims=True)
        o = o * lax.rsqrt(ms + RMS_EPS) * g_ref[:, sl]
        o_ref[0, rs, sl] = (o * z_ref[0, rs, sl].astype(F32)).astype(BF16)
        if after_head is not None:
            after_head(h)


def _fox_kernel(q_ref, k_ref, v_ref, c_ref, z_ref, o_ref, vaug_ref, m_ref, acc_ref, *, tq, tk):
    h = pl.program_id(1)
    qi = pl.program_id(2)

    @pl.when(qi == 0)
    def _():
        vaug_ref[:, :HEAD_DK] = v_ref[0]
        vaug_ref[:, HEAD_DK:] = jnp.ones((vaug_ref.shape[0], HEAD_DK), BF16)

    per_q = tq // tk

    def c_row(j):
        return c_ref[0, j, pl.ds(h, 1), :] * LOG2E

    cref = c_row(qi * per_q)[:, :1]
    m_ref[...] = jnp.full_like(m_ref, -jnp.inf)
    acc_ref[...] = jnp.zeros_like(acc_ref)

    def chunk(j, r0, masked):
        ks = pl.multiple_of(j * tk, tk)
        s = lax.dot_general(q_ref[0, r0:, :], k_ref[0, pl.ds(ks, tk), :], (((1,), (1,)), ((), ())),
                            preferred_element_type=F32)
        s = s + (cref - c_row(j))
        if masked:
            row = lax.broadcasted_iota(jnp.int32, s.shape, 0)
            col = lax.broadcasted_iota(jnp.int32, s.shape, 1)
            s = jnp.where(col <= row, s, -jnp.inf)
        m_old = m_ref[r0:, :]
        m_new = jnp.maximum(m_old, jnp.max(s, axis=-1, keepdims=True))
        alpha = jnp.exp2(m_old - m_new)
        p = jnp.exp2(s - jnp.tile(m_new, (1, tk // 128))).astype(BF16)
        acc_ref[r0:, :] = (jnp.tile(alpha, (1, 2)) * acc_ref[r0:, :]
                           + jnp.dot(p, vaug_ref[pl.ds(ks, tk), :], preferred_element_type=F32))
        m_ref[r0:, :] = m_new

    def body(i, carry):
        for u in range(per_q):
            chunk(i * per_q + u, 0, False)
        return carry

    lax.fori_loop(0, qi, body, 0)
    for u in range(per_q):
        chunk(qi * per_q + u, u * tk, True)
    acc = acc_ref[...]
    o = acc[:, :HEAD_DK] / acc[:, HEAD_DK:]
    o_ref[0] = (o * z_ref[0].astype(F32)).astype(BF16)


def _fox(q, k, v, c4, z, *, tq):
    B, S, W = q.shape
    nk, tk = c4.shape[1], c4.shape[3]
    nq = S // tq
    qspec = pl.BlockSpec((1, tq, HEAD_DK), lambda b, h, i: (b, i, h))
    kspec = pl.BlockSpec((1, S, HEAD_DK), lambda b, h, i: (b, 0, h))
    return pl.pallas_call(
        functools.partial(_fox_kernel, tq=tq, tk=tk),
        grid=(B, HEADS, nq),
        in_specs=[qspec, kspec, kspec,
                  pl.BlockSpec((1, nk, GATE_ROWS, tk), lambda b, h, i: (b, 0, 0, 0)),
                  qspec],
        out_specs=qspec,
        out_shape=jax.ShapeDtypeStruct((B, S, W), BF16),
        scratch_shapes=[pltpu.VMEM((S, 2 * HEAD_DK), BF16), pltpu.VMEM((tq, 128), F32),
                        pltpu.VMEM((tq, 2 * HEAD_DK), F32)],
        compiler_params=_cparams(("parallel", "parallel", "arbitrary")),
        name="fox_attention",
    )(q, k, v, c4, z)


def _outproj_ln_kernel(*refs, n_act):
    act_refs = refs[:n_act]
    w_ref, x_ref, g_ref, b_ref, o_ref = refs[n_act:]
    y = ALPHA * x_ref[...]
    off = 0
    for a_ref in act_refs:
        wd = a_ref.shape[1]
        y = y + jnp.dot(a_ref[...], w_ref[off:off + wd, :], preferred_element_type=F32)
        off += wd
    mu = jnp.mean(y, axis=-1, keepdims=True)
    yc = y - mu
    var = jnp.mean(yc * yc, axis=-1, keepdims=True)
    o_ref[...] = yc * lax.rsqrt(var + LN_EPS) * g_ref[...] + b_ref[...]


def _outproj_ln(acts, w, x, g, b, *, tm):
    N, D = x.shape
    tok = lambda wd: pl.BlockSpec((tm, wd), lambda i: (i, 0))
    const = lambda shape: pl.BlockSpec(shape, lambda i: (0,) * len(shape))
    return pl.pallas_call(
        functools.partial(_outproj_ln_kernel, n_act=len(acts)),
        grid=(N // tm,),
        in_specs=[tok(a.shape[1]) for a in acts] + [const(w.shape), tok(D), const(g.shape), const(b.shape)],
        out_specs=tok(D),
        out_shape=jax.ShapeDtypeStruct((N, D), F32),
        compiler_params=_cparams(("parallel",)),
        name="outproj_ln",
    )(*acts, w, x, g, b)


def _inproj_c_kernel(x_ref, w_ref, wb_ref, wg_ref, cw_ref, cb_ref, gb_ref,
                     q_ref, k_ref, v_ref, og_ref, z_ref, gt_ref,
                     ubuf_ref, *, tm, kw, vw, k_scale):
    si = pl.program_id(1)
    xb = x_ref[0].astype(BF16)

    g = lax.dot_general(wg_ref[...], xb, (((1,), (1,)), ((), ())), preferred_element_type=F32) + gb_ref[...]
    rowi = lax.broadcasted_iota(jnp.int32, g.shape, 0)
    b = _cumsum_lanes(_log_sigmoid(g), seg=ML_CHUNK)
    gt_ref[0] = jnp.where(rowi < HEADS, g - pltpu.roll(b, HEADS, 0), b)

    def mm(lo, hi):
        return _mm_split(xb, w_ref, wb_ref, lo, hi - lo)

    pad = 8

    @pl.when(si == 0)
    def _():
        ubuf_ref[0:pad, :] = jnp.zeros((pad, 2 * kw), F32)

    cwid = 256
    o0 = 2 * kw

    def conv_chunk(c0):
        cs = slice(c0, c0 + cwid)
        ubuf_ref[pad:pad + tm, cs] = mm(c0, c0 + cwid)
        acc = cb_ref[:, cs]
        for j in range(ML_CONV):
            st = pad - (ML_CONV - 1) + j
            acc = acc + cw_ref[j:j + 1, cs] * ubuf_ref[st:st + tm, cs]
        ubuf_ref[0:pad, cs] = ubuf_ref[tm:tm + pad, cs]
        if c0 < kw:
            q_ref[0, :, cs] = _silu(acc).astype(BF16)
        else:
            k_ref[0, :, c0 - kw:c0 - kw + cwid] = (_silu(acc) * k_scale).astype(BF16)

    assert 2 * kw == vw
    for c0 in range(0, vw, cwid):
        cs = slice(c0, c0 + cwid)
        conv_chunk(c0)
        v_ref[0, :, cs] = mm(o0 + c0, o0 + c0 + cwid).astype(BF16)
        og_ref[0, :, cs] = _sigmoid(mm(o0 + vw + c0, o0 + vw + c0 + cwid)).astype(BF16)
        z_ref[0, :, cs] = _silu(mm(o0 + 2 * vw + c0, o0 + 2 * vw + c0 + cwid)).astype(BF16)


def _inproj_c(x, w, wb, wg, cw, cb, gb, *, tm):
    B, S, D = x.shape
    kw = cw.shape[1] // 2
    vw = (w.shape[1] + wb.shape[1] - 2 * kw) // 3
    const = lambda shape: pl.BlockSpec(shape, lambda b, s: (0,) * len(shape))
    tok = lambda wd: pl.BlockSpec((1, tm, wd), lambda b, s: (b, s, 0))
    return pl.pallas_call(
        functools.partial(_inproj_c_kernel, tm=tm, kw=kw, vw=vw, k_scale=HEAD_DK ** -0.5),
        grid=(B, S // tm),
        in_specs=[pl.BlockSpec((1, tm, D), lambda b, s: (b, s, 0)),
                  const(w.shape), const(wb.shape), const(wg.shape), const(cw.shape), const(cb.shape),
                  const(gb.shape)],
        out_specs=[tok(kw), tok(kw), tok(vw), tok(vw), tok(vw),
                   pl.BlockSpec((1, GATE_ROWS, tm), lambda b, s: (b, 0, s))],
        out_shape=[jax.ShapeDtypeStruct((B, S, kw), BF16), jax.ShapeDtypeStruct((B, S, kw), BF16),
                   jax.ShapeDtypeStruct((B, S, vw), BF16), jax.ShapeDtypeStruct((B, S, vw), BF16),
                   jax.ShapeDtypeStruct((B, S, vw), BF16), jax.ShapeDtypeStruct((B, GATE_ROWS, S), F32)],
        scratch_shapes=[pltpu.VMEM((tm + 8, 2 * kw), F32)],
        compiler_params=_cparams(("parallel", "arbitrary")),
        name="inproj_c",
    )(x, w, wb, wg, cw, cb, gb)


def _mlstm_kernel(q_ref, k_ref, v_ref, gt_ref, og_ref, z_ref, g_ref, o_ref, c_ref, m_ref, *, L):
    @pl.when(pl.program_id(1) == 0)
    def _():
        c_ref[...] = jnp.zeros_like(c_ref)
        m_ref[...] = jnp.zeros_like(m_ref)

    for sub in range(q_ref.shape[1] // L):
        for h in range(HEADS):
            _mlstm_head(h, slice(sub * L, (sub + 1) * L),
                        q_ref, k_ref, v_ref, gt_ref, og_ref, z_ref, g_ref, o_ref, c_ref, m_ref, L=L)


def _mlstm_head(h, rs, q_ref, k_ref, v_ref, gt_ref, og_ref, z_ref, g_ref, o_ref, c_ref, m_ref, *, L):
    row = lax.broadcasted_iota(jnp.int32, (L, L), 0)
    col = lax.broadcasted_iota(jnp.int32, (L, L), 1)
    tril = col <= row
    eye = col == row
    ks = slice(h * HEAD_DK, (h + 1) * HEAD_DK)
    vs = slice(h * ML_DV, (h + 1) * ML_DV)
    q = q_ref[0, rs, ks]
    k = k_ref[0, rs, ks]
    v = v_ref[0, rs, vs]
    a_row = gt_ref[0, h:h + 1, rs]
    b_row = gt_ref[0, HEADS + h:HEADS + h + 1, rs]
    m_prev = m_ref[h:h + 1, :1]

    M_col = jnp.maximum(m_prev, jnp.max(jnp.where(tril, a_row, -jnp.inf), axis=-1, keepdims=True))
    b_col = jnp.sum(jnp.where(eye, b_row, 0.0), axis=-1, keepdims=True)
    w = jnp.where(tril, jnp.exp(a_row - M_col), 0.0)
    s_inter = jnp.exp(m_prev - M_col)

    v_ext = jnp.concatenate([v, jnp.ones((L, HEAD_DK), BF16)], axis=1)
    qk = lax.dot_general(q, k, (((1,), (1,)), ((), ())), preferred_element_type=F32) * w
    ne = (jnp.dot(qk.astype(BF16), v_ext, preferred_element_type=F32)
          + s_inter * jnp.dot(q, c_ref[h].astype(BF16), preferred_element_type=F32))
    den = jnp.maximum(jnp.abs(ne[:, ML_DV:]), jnp.exp(-(b_col + M_col)))
    hval = ne[:, :ML_DV] / jnp.tile(den, (1, ML_DV // HEAD_DK))

    M_last = jnp.maximum(m_prev, jnp.max(a_row, axis=-1, keepdims=True))
    wk_row = jnp.exp(a_row - M_last)
    decay = jnp.exp(m_prev - M_last)
    wk_col = jnp.sum(jnp.where(eye, wk_row, 0.0), axis=-1, keepdims=True)
    kw = (k.astype(F32) * wk_col).astype(BF16)
    c_ref[h] = decay * c_ref[h] + lax.dot_general(kw, v_ext, (((0,), (0,)), ((), ())),
                                                  preferred_element_type=F32)
    m_ref[h:h + 1, :] = jnp.broadcast_to(b_row[:, L - 1:] + M_last, (1, m_ref.shape[1]))

    ht = og_ref[0, rs, vs].astype(F32) * hval
    ms = jnp.mean(ht * ht, axis=-1, keepdims=True)
    ht = ht * lax.rsqrt(ms + RMS_EPS) * g_ref[:, vs]
    o_ref[0, rs, vs] = (ht * z_ref[0, rs, vs].astype(F32)).astype(BF16)


def _mlstm(q, k, v, gt, og, z, g, *, L, T):
    B, S, KW = q.shape
    W = v.shape[2]
    kspec = pl.BlockSpec((1, T, KW), lambda b, c: (b, c, 0))
    vspec = pl.BlockSpec((1, T, W), lambda b, c: (b, c, 0))
    return pl.pallas_call(
        functools.partial(_mlstm_kernel, L=L),
        grid=(B, S // T),
        in_specs=[kspec, kspec, vspec,
                  pl.BlockSpec((1, GATE_ROWS, T), lambda b, c: (b, 0, c)),
                  vspec, vspec, pl.BlockSpec((1, W), lambda b, c: (0, 0))],
        out_specs=vspec,
        out_shape=jax.ShapeDtypeStruct((B, S, W), BF16),
        scratch_shapes=[pltpu.VMEM((HEADS, HEAD_DK, ML_DV + HEAD_DK), F32),
                        pltpu.VMEM((GATE_ROWS, 128), F32)],
        compiler_params=_cparams(("parallel", "arbitrary")),
        name="mlstm",
    )(q, k, v, gt, og, z, g)


def _gate_rows(w_in, start, n):
    wt = w_in[:, start:start + 128].T[:n]
    return jnp.pad(wt, ((0, GATE_ROWS - n), (0, 0))).astype(BF16)


def _split_cast(w_in, start, n):
    return w_in[:, :start].astype(BF16), w_in[:, start + n:].astype(BF16)


def kernel(x, hgrn_lb_logits, ab_w_in, ab_fox_bf, ab_hgrn_norm_g, ab_w_out, c_w_in,
           c_conv_w, c_conv_b, c_bi, c_bf, c_norm_g, c_w_out, ln_g, ln_b):
    B, S, D = x.shape
    W = HEADS * HEAD_DK
    TM = min(1024, S)
    TM_AB = min(512, S)

    lb_table = jnp.cumsum(jax.nn.softmax(hgrn_lb_logits.astype(F32), axis=0), axis=0)
    h = x.astype(F32)

    w_in = ab_w_in[0].astype(F32)
    w_a, w_b = _split_cast(w_in, 7 * W, HEADS)
    w_gate = _gate_rows(w_in, 7 * W, HEADS)
    bfb = jnp.broadcast_to(jnp.pad(ab_fox_bf[0].astype(F32), (0, GATE_ROWS - HEADS))[:, None], (GATE_ROWS, TM_AB))
    hg, fq, fk, fv, fz, c = _inproj_ab(h, w_a, w_b, w_gate, lb_table[0][None, :], bfb,
                                       ab_hgrn_norm_g[0].astype(F32)[None, :], tm=TM_AB, T=min(256, S), ck=min(256, S))
    fx = _fox(fq, fk, fv, c, fz, tq=min(2048, S))
    h = _outproj_ln([hg.reshape(B * S, W), fx.reshape(B * S, W)], ab_w_out[0].astype(BF16),
                    h.reshape(B * S, D), ln_g[0].astype(F32)[None, :], ln_b[0].astype(F32)[None, :],
                    tm=TM).reshape(B, S, D)

    w_in = c_w_in[0].astype(F32)
    KW, VW = W, HEADS * ML_DV
    o_i = 2 * KW + VW
    w_a, w_b = _split_cast(w_in, o_i, 2 * HEADS)
    w_gate = _gate_rows(w_in, o_i, 2 * HEADS)
    gb = jnp.broadcast_to(jnp.concatenate([c_bi[0], c_bf[0]]).astype(F32)[:, None], (GATE_ROWS, TM))
    mq, mk, mv, og, mz, gt = _inproj_c(h, w_a, w_b, w_gate, c_conv_w[0].astype(F32),
                                       c_conv_b[0].astype(F32)[None, :], gb, tm=TM)
    ht = _mlstm(mq, mk, mv, gt, og, mz, c_norm_g[0].astype(F32)[None, :], L=ML_CHUNK, T=min(4 * ML_CHUNK, S))
    h = _outproj_ln([ht.reshape(B * S, VW)], c_w_out[0].astype(BF16), h.reshape(B * S, D),
                    ln_g[1].astype(F32)[None, :], ln_b[1].astype(F32)[None, :], tm=TM).reshape(B, S, D)
    return h.astype(x.dtype)
```

```python
import functools
import math

import jax
import jax.numpy as jnp
from jax import lax
from jax.experimental import pallas as pl
from jax.experimental.pallas import tpu as pltpu

F32 = jnp.float32
BF16 = jnp.bfloat16

DEPTH = 2
ALPHA = (2 * DEPTH) ** 0.25
LN_EPS = 1e-5
RMS_EPS = 1e-6

HEADS = 4
HEAD_DK = 128
ML_DV = 256
HG_CHUNK = 32
ML_CONV = 4
ML_CHUNK = 256
GATE_ROWS = 8

VMEM_LIMIT = 56 * 1024 * 1024
LOG2E = math.log2(math.e)


def _cparams(sem):
    return pltpu.CompilerParams(dimension_semantics=sem, vmem_limit_bytes=VMEM_LIMIT)


def _sigmoid(z):
    return 0.5 + 0.5 * jnp.tanh(0.5 * z)


def _silu(z):
    hz = 0.5 * z
    return hz + hz * jnp.tanh(hz)


def _log_sigmoid(z):
    return jnp.minimum(z, 0.0) - jnp.log(1.0 + jnp.exp(-jnp.abs(z)))


def _cumsum_lanes(x, seg=None):
    seg = seg or x.shape[-1]
    pos = lax.broadcasted_iota(jnp.int32, x.shape, x.ndim - 1) % seg
    sh = 1
    while sh < seg:
        x = x + jnp.where(pos >= sh, pltpu.roll(x, sh, x.ndim - 1), 0.0)
        sh *= 2
    return x


def _mm_split(x, w_ref, wb_ref, lo, n):
    na = w_ref.shape[1]
    if lo + n <= na:
        return jnp.dot(x, w_ref[:, lo:lo + n], preferred_element_type=F32)
    assert lo >= na
    return jnp.dot(x, wb_ref[:, lo - na:lo - na + n], preferred_element_type=F32)


def _inproj_ab_kernel(x_ref, w_ref, wb_ref, wg_ref, lb_ref, bf_ref, g_ref,
                      hg_ref, fq_ref, fk_ref, fv_ref, fz_ref, c_ref,
                      carry_ref, st_ref, xb_ref, q_ref, lf_ref, k_ref, v_ref, z_ref,
                      *, width, fq_scale, T, piece):
    si = pl.program_id(1)
    xb_ref[...] = x_ref[0].astype(BF16)
    xb = xb_ref[...]

    g = lax.dot_general(wg_ref[...], xb, (((1,), (1,)), ((), ())), preferred_element_type=F32)
    ls = _log_sigmoid(g + bf_ref[...])

    @pl.when(si == 0)
    def _():
        carry_ref[...] = jnp.zeros_like(carry_ref)
        st_ref[...] = jnp.zeros_like(st_ref)

    c = _cumsum_lanes(ls) + carry_ref[:, :1]
    ck = c_ref.shape[3]
    for j in range(c_ref.shape[1]):
        c_ref[0, j] = c[:, j * ck:(j + 1) * ck]
    carry_ref[...] = jnp.broadcast_to(c[:, -1:], carry_ref.shape)

    def mm(lo, n):
        return _mm_split(xb_ref[...], w_ref, wb_ref, lo, n)

    q_ref[0] = mm(0, width).astype(BF16)
    lb = lb_ref[...]
    f = lb + (1.0 - lb) * _sigmoid(mm(width, width))
    lf_ref[0] = jnp.log(f)
    k_ref[0] = (1.0 - f).astype(BF16)
    v_ref[0] = mm(2 * width, width).astype(BF16)
    z_ref[0] = _silu(mm(3 * width, width)).astype(BF16)

    fox_out = (fq_ref, fk_ref, fv_ref, fz_ref)
    per_group = width // piece

    def fox_piece(n):
        grp, part = n // per_group, n % per_group
        cs = slice(part * piece, (part + 1) * piece)
        y = mm((4 + grp) * width + part * piece, piece)
        if grp == 0:
            y = y * fq_scale
        elif grp == 3:
            y = _silu(y)
        fox_out[grp][0, :, cs] = y.astype(BF16)

    n_sub = x_ref.shape[1] // T
    stride = n_sub * HEADS * piece // (4 * width)

    def after_head(h, sub):
        n = sub * HEADS + h
        if n % stride == stride - 1:
            fox_piece(n // stride)

    for sub in range(n_sub):
        _hgrn2_tile(slice(sub * T, (sub + 1) * T), q_ref, lf_ref, k_ref, v_ref, z_ref, g_ref, hg_ref, st_ref,
                    T=T, after_head=functools.partial(after_head, sub=sub))


def _inproj_ab(x, w, wb, wg, lb, bfb, g, *, tm, T, ck):
    B, S, D = x.shape
    width = (w.shape[1] + wb.shape[1]) // 8
    piece = max(256, 4 * width // ((tm // T) * HEADS))
    assert width % piece == 0 and ((tm // T) * HEADS * piece) % (4 * width) == 0
    wide = jax.ShapeDtypeStruct((B, S, width), BF16)
    tok = pl.BlockSpec((1, tm, width), lambda b, s: (b, s, 0))
    const = lambda shape: pl.BlockSpec(shape, lambda b, s: (0,) * len(shape))
    return pl.pallas_call(
        functools.partial(_inproj_ab_kernel, width=width, fq_scale=HEAD_DK ** -0.5 * LOG2E, T=T, piece=piece),
        grid=(B, S // tm),
        in_specs=[pl.BlockSpec((1, tm, D), lambda b, s: (b, s, 0)),
                  const(w.shape), const(wb.shape), const(wg.shape), const(lb.shape), const(bfb.shape),
                  const(g.shape)],
        out_specs=[tok, tok, tok, tok, tok,
                   pl.BlockSpec((1, tm // ck, GATE_ROWS, ck), lambda b, s: (b, s, 0, 0))],
        out_shape=[wide, wide, wide, wide, wide,
                   jax.ShapeDtypeStruct((B, S // ck, GATE_ROWS, ck), F32)],
        scratch_shapes=[pltpu.VMEM((GATE_ROWS, 128), F32), pltpu.VMEM((HEADS, HEAD_DK, HEAD_DK), F32),
                        pltpu.VMEM((tm, D), BF16),
                        pltpu.VMEM((1, tm, width), BF16), pltpu.VMEM((1, tm, width), F32),
                        pltpu.VMEM((1, tm, width), BF16), pltpu.VMEM((1, tm, width), BF16),
                        pltpu.VMEM((1, tm, width), BF16)],
        compiler_params=_cparams(("parallel", "arbitrary")),
        name="inproj_ab_hgrn2",
    )(x, w, wb, wg, lb, bfb, g)


def _hgrn2_tile(rs, q_ref, lf_ref, k_ref, v_ref, z_ref, g_ref, o_ref, st_ref, *, T, after_head=None):
    L = HG_CHUNK
    BL = 2 * L
    row = lax.broadcasted_iota(jnp.int32, (T, T), 0)
    col = lax.broadcasted_iota(jnp.int32, (T, T), 1)
    rc, cc = row // L, col // L
    same_chunk = rc == cc
    same_chunk_causal = same_chunk & (col <= row)
    prev_chunk_in_block = (rc == cc + 1) & (rc % 2 == 1)
    tri = jnp.where(same_chunk_causal, 1.0, 0.0).astype(BF16)
    last = jnp.where(same_chunk, 1.0, 0.0).astype(BF16)
    odd_row = (lax.broadcasted_iota(jnp.int32, (T, HEAD_DK), 0) // L) % 2 == 1

    lf_all = lf_ref[0, rs, :]
    lf_hi = lf_all.astype(BF16)
    lf_lo = (lf_all - lf_hi.astype(F32)).astype(BF16)
    b_all = (jnp.dot(tri, lf_hi, preferred_element_type=F32)
             + jnp.dot(tri, lf_lo, preferred_element_type=F32))
    btot_all = (jnp.dot(last, lf_hi, preferred_element_type=F32)
                + jnp.dot(last, lf_lo, preferred_element_type=F32))

    for h in range(HEADS):
        sl = slice(h * HEAD_DK, (h + 1) * HEAD_DK)
        b = b_all[:, sl]
        btot = btot_all[:, sl]
        q = q_ref[0, rs, sl].astype(F32)
        k = k_ref[0, rs, sl].astype(F32)
        v = v_ref[0, rs, sl]
        dec_tot = jnp.exp(btot)
        dec_prev = pltpu.roll(dec_tot, L, 0)
        dec_next = pltpu.roll(dec_tot, T - L, 0)
        q_dec32 = q * jnp.exp(b)
        k_end32 = k * jnp.exp(btot - b)
        q_dec = q_dec32.astype(BF16)
        k_inv = (k * jnp.exp(-b)).astype(BF16)
        k_end = k_end32.astype(BF16)
        q_blk = jnp.where(odd_row, q_dec32 * dec_prev, q_dec32).astype(BF16)
        k_blk = jnp.where(odd_row, k_end32, k_end32 * dec_next).astype(BF16)
        dec_blk = dec_tot * dec_next

        nt = (((1,), (1,)), ((), ()))
        a_in = lax.dot_general(q_dec, k_inv, nt, preferred_element_type=F32)
        a_x = lax.dot_general(q_dec, k_end, nt, preferred_element_type=F32)
        a = jnp.where(same_chunk_causal, a_in, jnp.where(prev_chunk_in_block, a_x, 0.0)).astype(BF16)
        o = jnp.dot(a, v, preferred_element_type=F32)

        st = st_ref[h]
        inter = []
        for c in range(T // BL):
            bs = slice(c * BL, (c + 1) * BL)
            inter.append(lax.dot_general(q_blk[bs], st.astype(BF16), nt, preferred_element_type=F32))
            upd = lax.dot_general(v[bs], k_blk[bs], (((0,), (0,)), ((), ())),
                                  preferred_element_type=F32)
            st = st * dec_blk[c * BL:c * BL + 1, :] + upd
        st_ref[h] = st
        o = o + jnp.concatenate(inter, axis=0)

        ms = jnp.mean(o * o, axis=-1, keepdims=True)
        o = o * lax.rsqrt(ms + RMS_EPS) * g_ref[:, sl]
        o_ref[0, rs, sl] = (o * z_ref[0, rs, sl].astype(F32)).astype(BF16)
        if after_head is not None:
            after_head(h)


def _fox_kernel(q_ref, k_ref, v_ref, c_ref, z_ref, o_ref, vaug_ref, m_ref, acc_ref, *, tq, tk):
    h = pl.program_id(1)
    qi = pl.program_id(2)

    @pl.when(qi == 0)
    def _():
        vaug_ref[:, :HEAD_DK] = v_ref[0]
        vaug_ref[:, HEAD_DK:] = jnp.ones((vaug_ref.shape[0], HEAD_DK), BF16)

    per_q = tq // tk

    def c_row(j):
        return c_ref[0, j, pl.ds(h, 1), :] * LOG2E

    cref = c_row(qi * per_q)[:, :1]
    m_ref[...] = jnp.full_like(m_ref, -jnp.inf)
    acc_ref[...] = jnp.zeros_like(acc_ref)

    def chunk(j, r0, masked):
        ks = pl.multiple_of(j * tk, tk)
        s = lax.dot_general(q_ref[0, r0:, :], k_ref[0, pl.ds(ks, tk), :], (((1,), (1,)), ((), ())),
                            preferred_element_type=F32)
        s = s + (cref - c_row(j))
        if masked:
            row = lax.broadcasted_iota(jnp.int32, s.shape, 0)
            col = lax.broadcasted_iota(jnp.int32, s.shape, 1)
            s = jnp.where(col <= row, s, -jnp.inf)
        m_old = m_ref[r0:, :]
        m_new = jnp.maximum(m_old, jnp.max(s, axis=-1, keepdims=True))
        alpha = jnp.exp2(m_old - m_new)
        p = jnp.exp2(s - jnp.tile(m_new, (1, tk // 128))).astype(BF16)
        acc_ref[r0:, :] = (jnp.tile(alpha, (1, 2)) * acc_ref[r0:, :]
                           + jnp.dot(p, vaug_ref[pl.ds(ks, tk), :], preferred_element_type=F32))
        m_ref[r0:, :] = m_new

    def body(i, carry):
        for u in range(per_q):
            chunk(i * per_q + u, 0, False)
        return carry

    lax.fori_loop(0, qi, body, 0)
    for u in range(per_q):
        chunk(qi * per_q + u, u * tk, True)
    acc = acc_ref[...]
    o = acc[:, :HEAD_DK] / acc[:, HEAD_DK:]
    o_ref[0] = (o * z_ref[0].astype(F32)).astype(BF16)


def _fox(q, k, v, c4, z, *, tq):
    B, S, W = q.shape
    nk, tk = c4.shape[1], c4.shape[3]
    nq = S // tq
    qspec = pl.BlockSpec((1, tq, HEAD_DK), lambda b, h, i: (b, i, h))
    kspec = pl.BlockSpec((1, S, HEAD_DK), lambda b, h, i: (b, 0, h))
    return pl.pallas_call(
        functools.partial(_fox_kernel, tq=tq, tk=tk),
        grid=(B, HEADS, nq),
        in_specs=[qspec, kspec, kspec,
                  pl.BlockSpec((1, nk, GATE_ROWS, tk), lambda b, h, i: (b, 0, 0, 0)),
                  qspec],
        out_specs=qspec,
        out_shape=jax.ShapeDtypeStruct((B, S, W), BF16),
        scratch_shapes=[pltpu.VMEM((S, 2 * HEAD_DK), BF16), pltpu.VMEM((tq, 128), F32),
                        pltpu.VMEM((tq, 2 * HEAD_DK), F32)],
        compiler_params=_cparams(("parallel", "parallel", "arbitrary")),
        name="fox_attention",
    )(q, k, v, c4, z)


def _outproj_ln_kernel(*refs, n_act):
    act_refs = refs[:n_act]
    w_ref, x_ref, g_ref, b_ref, o_ref = refs[n_act:]
    y = ALPHA * x_ref[...]
    off = 0
    for a_ref in act_refs:
        wd = a_ref.shape[1]
        y = y + jnp.dot(a_ref[...], w_ref[off:off + wd, :], preferred_element_type=F32)
        off += wd
    mu = jnp.mean(y, axis=-1, keepdims=True)
    yc = y - mu
    var = jnp.mean(yc * yc, axis=-1, keepdims=True)
    o_ref[...] = yc * lax.rsqrt(var + LN_EPS) * g_ref[...] + b_ref[...]


def _outproj_ln(acts, w, x, g, b, *, tm):
    N, D = x.shape
    tok = lambda wd: pl.BlockSpec((tm, wd), lambda i: (i, 0))
    const = lambda shape: pl.BlockSpec(shape, lambda i: (0,) * len(shape))
    return pl.pallas_call(
        functools.partial(_outproj_ln_kernel, n_act=len(acts)),
        grid=(N // tm,),
        in_specs=[tok(a.shape[1]) for a in acts] + [const(w.shape), tok(D), const(g.shape), const(b.shape)],
        out_specs=tok(D),
        out_shape=jax.ShapeDtypeStruct((N, D), F32),
        compiler_params=_cparams(("parallel",)),
        name="outproj_ln",
    )(*acts, w, x, g, b)


def _inproj_c_kernel(x_ref, w_ref, wb_ref, wg_ref, cw_ref, cb_ref, gb_ref,
                     q_ref, k_ref, v_ref, og_ref, z_ref, gt_ref,
                     ubuf_ref, *, tm, kw, vw, k_scale):
    si = pl.program_id(1)
    xb = x_ref[0].astype(BF16)

    g = lax.dot_general(wg_ref[...], xb, (((1,), (1,)), ((), ())), preferred_element_type=F32) + gb_ref[...]
    rowi = lax.broadcasted_iota(jnp.int32, g.shape, 0)
    b = _cumsum_lanes(_log_sigmoid(g), seg=ML_CHUNK)
    gt_ref[0] = jnp.where(rowi < HEADS, g - pltpu.roll(b, HEADS, 0), b)

    def mm(lo, hi):
        return _mm_split(xb, w_ref, wb_ref, lo, hi - lo)

    pad = 8

    @pl.when(si == 0)
    def _():
        ubuf_ref[0:pad, :] = jnp.zeros((pad, 2 * kw), F32)

    cwid = 256
    o0 = 2 * kw

    def conv_chunk(c0):
        cs = slice(c0, c0 + cwid)
        ubuf_ref[pad:pad + tm, cs] = mm(c0, c0 + cwid)
        acc = cb_ref[:, cs]
        for j in range(ML_CONV):
            st = pad - (ML_CONV - 1) + j
            acc = acc + cw_ref[j:j + 1, cs] * ubuf_ref[st:st + tm, cs]
        ubuf_ref[0:pad, cs] = ubuf_ref[tm:tm + pad, cs]
        if c0 < kw:
            q_ref[0, :, cs] = _silu(acc).astype(BF16)
        else:
            k_ref[0, :, c0 - kw:c0 - kw + cwid] = (_silu(acc) * k_scale).astype(BF16)

    assert 2 * kw == vw
    for c0 in range(0, vw, cwid):
        cs = slice(c0, c0 + cwid)
        conv_chunk(c0)
        v_ref[0, :, cs] = mm(o0 + c0, o0 + c0 + cwid).astype(BF16)
        og_ref[0, :, cs] = _sigmoid(mm(o0 + vw + c0, o0 + vw + c0 + cwid)).astype(BF16)
        z_ref[0, :, cs] = _silu(mm(o0 + 2 * vw + c0, o0 + 2 * vw + c0 + cwid)).astype(BF16)


def _inproj_c(x, w, wb, wg, cw, cb, gb, *, tm):
    B, S, D = x.shape
    kw = cw.shape[1] // 2
    vw = (w.shape[1] + wb.shape[1] - 2 * kw) // 3
    const = lambda shape: pl.BlockSpec(shape, lambda b, s: (0,) * len(shape))
    tok = lambda wd: pl.BlockSpec((1, tm, wd), lambda b, s: (b, s, 0))
    return pl.pallas_call(
        functools.partial(_inproj_c_kernel, tm=tm, kw=kw, vw=vw, k_scale=HEAD_DK ** -0.5),
        grid=(B, S // tm),
        in_specs=[pl.BlockSpec((1, tm, D), lambda b, s: (b, s, 0)),
                  const(w.shape), const(wb.shape), const(wg.shape), const(cw.shape), const(cb.shape),
                  const(gb.shape)],
        out_specs=[tok(kw), tok(kw), tok(vw), tok(vw), tok(vw),
                   pl.BlockSpec((1, GATE_ROWS, tm), lambda b, s: (b, 0, s))],
        out_shape=[jax.ShapeDtypeStruct((B, S, kw), BF16), jax.ShapeDtypeStruct((B, S, kw), BF16),
                   jax.ShapeDtypeStruct((B, S, vw), BF16), jax.ShapeDtypeStruct((B, S, vw), BF16),
                   jax.ShapeDtypeStruct((B, S, vw), BF16), jax.ShapeDtypeStruct((B, GATE_ROWS, S), F32)],
        scratch_shapes=[pltpu.VMEM((tm + 8, 2 * kw), F32)],
        compiler_params=_cparams(("parallel", "arbitrary")),
        name="inproj_c",
    )(x, w, wb, wg, cw, cb, gb)


def _mlstm_kernel(q_ref, k_ref, v_ref, gt_ref, og_ref, z_ref, g_ref, o_ref, c_ref, m_ref, *, L):
    @pl.when(pl.program_id(1) == 0)
    def _():
        c_ref[...] = jnp.zeros_like(c_ref)
        m_ref[...] = jnp.zeros_like(m_ref)

    for sub in range(q_ref.shape[1] // L):
        for h in range(HEADS):
            _mlstm_head(h, slice(sub * L, (sub + 1) * L),
                        q_ref, k_ref, v_ref, gt_ref, og_ref, z_ref, g_ref, o_ref, c_ref, m_ref, L=L)


def _mlstm_head(h, rs, q_ref, k_ref, v_ref, gt_ref, og_ref, z_ref, g_ref, o_ref, c_ref, m_ref, *, L):
    row = lax.broadcasted_iota(jnp.int32, (L, L), 0)
    col = lax.broadcasted_iota(jnp.int32, (L, L), 1)
    tril = col <= row
    eye = col == row
    ks = slice(h * HEAD_DK, (h + 1) * HEAD_DK)
    vs = slice(h * ML_DV, (h + 1) * ML_DV)
    q = q_ref[0, rs, ks]
    k = k_ref[0, rs, ks]
    v = v_ref[0, rs, vs]
    a_row = gt_ref[0, h:h + 1, rs]
    b_row = gt_ref[0, HEADS + h:HEADS + h + 1, rs]
    m_prev = m_ref[h:h + 1, :1]

    M_col = jnp.maximum(m_prev, jnp.max(jnp.where(tril, a_row, -jnp.inf), axis=-1, keepdims=True))
    b_col = jnp.sum(jnp.where(eye, b_row, 0.0), axis=-1, keepdims=True)
    w = jnp.where(tril, jnp.exp(a_row - M_col), 0.0)
    s_inter = jnp.exp(m_prev - M_col)

    v_ext = jnp.concatenate([v, jnp.ones((L, HEAD_DK), BF16)], axis=1)
    qk = lax.dot_general(q, k, (((1,), (1,)), ((), ())), preferred_element_type=F32) * w
    ne = (jnp.dot(qk.astype(BF16), v_ext, preferred_element_type=F32)
          + s_inter * jnp.dot(q, c_ref[h].astype(BF16), preferred_element_type=F32))
    den = jnp.maximum(jnp.abs(ne[:, ML_DV:]), jnp.exp(-(b_col + M_col)))
    hval = ne[:, :ML_DV] / jnp.tile(den, (1, ML_DV // HEAD_DK))

    M_last = jnp.maximum(m_prev, jnp.max(a_row, axis=-1, keepdims=True))
    wk_row = jnp.exp(a_row - M_last)
    decay = jnp.exp(m_prev - M_last)
    wk_col = jnp.sum(jnp.where(eye, wk_row, 0.0), axis=-1, keepdims=True)
    kw = (k.astype(F32) * wk_col).astype(BF16)
    c_ref[h] = decay * c_ref[h] + lax.dot_general(kw, v_ext, (((0,), (0,)), ((), ())),
                                                  preferred_element_type=F32)
    m_ref[h:h + 1, :] = jnp.broadcast_to(b_row[:, L - 1:] + M_last, (1, m_ref.shape[1]))

    ht = og_ref[0, rs, vs].astype(F32) * hval
    ms = jnp.mean(ht * ht, axis=-1, keepdims=True)
    ht = ht * lax.rsqrt(ms + RMS_EPS) * g_ref[:, vs]
    o_ref[0, rs, vs] = (ht * z_ref[0, rs, vs].astype(F32)).astype(BF16)


def _mlstm(q, k, v, gt, og, z, g, *, L, T):
    B, S, KW = q.shape
    W = v.shape[2]
    kspec = pl.BlockSpec((1, T, KW), lambda b, c: (b, c, 0))
    vspec = pl.BlockSpec((1, T, W), lambda b, c: (b, c, 0))
    return pl.pallas_call(
        functools.partial(_mlstm_kernel, L=L),
        grid=(B, S // T),
        in_specs=[kspec, kspec, vspec,
                  pl.BlockSpec((1, GATE_ROWS, T), lambda b, c: (b, 0, c)),
                  vspec, vspec, pl.BlockSpec((1, W), lambda b, c: (0, 0))],
        out_specs=vspec,
        out_shape=jax.ShapeDtypeStruct((B, S, W), BF16),
        scratch_shapes=[pltpu.VMEM((HEADS, HEAD_DK, ML_DV + HEAD_DK), F32),
                        pltpu.VMEM((GATE_ROWS, 128), F32)],
        compiler_params=_cparams(("parallel", "arbitrary")),
        name="mlstm",
    )(q, k, v, gt, og, z, g)


def _gate_rows(w_in, start, n):
    wt = w_in[:, start:start + 128].T[:n]
    return jnp.pad(wt, ((0, GATE_ROWS - n), (0, 0))).astype(BF16)


def _split_cast(w_in, start, n):
    return w_in[:, :start].astype(BF16), w_in[:, start + n:].astype(BF16)


def kernel(x, hgrn_lb_logits, ab_w_in, ab_fox_bf, ab_hgrn_norm_g, ab_w_out, c_w_in,
           c_conv_w, c_conv_b, c_bi, c_bf, c_norm_g, c_w_out, ln_g, ln_b):
    B, S, D = x.shape
    W = HEADS * HEAD_DK
    TM = min(1024, S)
    TM_AB = min(1024, S)

    lb_table = jnp.cumsum(jax.nn.softmax(hgrn_lb_logits.astype(F32), axis=0), axis=0)
    h = x.astype(F32)

    w_in = ab_w_in[0].astype(F32)
    w_a, w_b = _split_cast(w_in, 7 * W, HEADS)
    w_gate = _gate_rows(w_in, 7 * W, HEADS)
    bfb = jnp.broadcast_to(jnp.pad(ab_fox_bf[0].astype(F32), (0, GATE_ROWS - HEADS))[:, None], (GATE_ROWS, TM_AB))
    hg, fq, fk, fv, fz, c = _inproj_ab(h, w_a, w_b, w_gate, lb_table[0][None, :], bfb,
                                       ab_hgrn_norm_g[0].astype(F32)[None, :], tm=TM_AB, T=min(256, S), ck=min(256, S))
    fx = _fox(fq, fk, fv, c, fz, tq=min(2048, S))
    h = _outproj_ln([hg.reshape(B * S, W), fx.reshape(B * S, W)], ab_w_out[0].astype(BF16),
                    h.reshape(B * S, D), ln_g[0].astype(F32)[None, :], ln_b[0].astype(F32)[None, :],
                    tm=TM).reshape(B, S, D)

    w_in = c_w_in[0].astype(F32)
    KW, VW = W, HEADS * ML_DV
    o_i = 2 * KW + VW
    w_a, w_b = _split_cast(w_in, o_i, 2 * HEADS)
    w_gate = _gate_rows(w_in, o_i, 2 * HEADS)
    gb = jnp.broadcast_to(jnp.concatenate([c_bi[0], c_bf[0]]).astype(F32)[:, None], (GATE_ROWS, TM))
    mq, mk, mv, og, mz, gt = _inproj_c(h, w_a, w_b, w_gate, c_conv_w[0].astype(F32),
                                       c_conv_b[0].astype(F32)[None, :], gb, tm=TM)
    ht = _mlstm(mq, mk, mv, gt, og, mz, c_norm_g[0].astype(F32)[None, :], L=ML_CHUNK, T=min(4 * ML_CHUNK, S))
    h = _outproj_ln([ht.reshape(B * S, VW)], c_w_out[0].astype(BF16), h.reshape(B * S, D),
                    ln_g[1].astype(F32)[None, :], ln_b[1].astype(F32)[None, :], tm=TM).reshape(B, S, D)
    return h.astype(x.dtype)
```

```python
import functools
import math

import jax
import jax.numpy as jnp
from jax import lax
from jax.experimental import pallas as pl
from jax.experimental.pallas import tpu as pltpu

F32 = jnp.float32
BF16 = jnp.bfloat16

DEPTH = 2
ALPHA = (2 * DEPTH) ** 0.25
LN_EPS = 1e-5
RMS_EPS = 1e-6

HEADS = 4
HEAD_DK = 128
ML_DV = 256
HG_CHUNK = 32
ML_CONV = 4
ML_CHUNK = 256

LANES = 128
SUBLANES = 8
MXU_COLS = 256
V7X_VMEM_BYTES = 64 * 1024 * 1024
VMEM_LIMIT = V7X_VMEM_BYTES * 7 // 8

GATE_ROWS = SUBLANES

PROJ_ROWS = 1024
HG_ROWS = 256
FOX_Q_ROWS = 2048
FOX_K_ROWS = 256
ML_ROWS = 4 * ML_CHUNK

LOG2E = math.log2(math.e)


def _cparams(sem):
    return pltpu.CompilerParams(dimension_semantics=sem, vmem_limit_bytes=VMEM_LIMIT)


def _sigmoid(z):
    return 0.5 + 0.5 * jnp.tanh(0.5 * z)


def _silu(z):
    hz = 0.5 * z
    return hz + hz * jnp.tanh(hz)


def _log_sigmoid(z):
    return jnp.minimum(z, 0.0) - jnp.log(1.0 + jnp.exp(-jnp.abs(z)))


def _cumsum_lanes(x, seg=None):
    seg = seg or x.shape[-1]
    pos = lax.broadcasted_iota(jnp.int32, x.shape, x.ndim - 1) % seg
    sh = 1
    while sh < seg:
        x = x + jnp.where(pos >= sh, pltpu.roll(x, sh, x.ndim - 1), 0.0)
        sh *= 2
    return x


def _mm_split(x, w_ref, wb_ref, lo, n):
    na = w_ref.shape[1]
    if lo + n <= na:
        return jnp.dot(x, w_ref[:, lo:lo + n], preferred_element_type=F32)
    assert lo >= na
    return jnp.dot(x, wb_ref[:, lo - na:lo - na + n], preferred_element_type=F32)


def _inproj_ab_kernel(x_ref, w_ref, wb_ref, wg_ref, lb_ref, bf_ref, g_ref,
                      hg_ref, fq_ref, fk_ref, fv_ref, fz_ref, c_ref,
                      carry_ref, st_ref, xb_ref, q_ref, lf_ref, k_ref, v_ref, z_ref,
                      *, width, fq_scale, T, piece):
    si = pl.program_id(1)
    xb_ref[...] = x_ref[0].astype(BF16)
    xb = xb_ref[...]

    g = lax.dot_general(wg_ref[...], xb, (((1,), (1,)), ((), ())), preferred_element_type=F32)
    ls = _log_sigmoid(g + bf_ref[...])

    @pl.when(si == 0)
    def _():
        carry_ref[...] = jnp.zeros_like(carry_ref)
        st_ref[...] = jnp.zeros_like(st_ref)

    c = _cumsum_lanes(ls) + carry_ref[:, :1]
    ck = c_ref.shape[3]
    for j in range(c_ref.shape[1]):
        c_ref[0, j] = c[:, j * ck:(j + 1) * ck]
    carry_ref[...] = jnp.broadcast_to(c[:, -1:], carry_ref.shape)

    def mm(lo, n):
        return _mm_split(xb_ref[...], w_ref, wb_ref, lo, n)

    q_ref[0] = mm(0, width).astype(BF16)
    lb = lb_ref[...]
    f = lb + (1.0 - lb) * _sigmoid(mm(width, width))
    lf_ref[0] = jnp.log(f)
    k_ref[0] = (1.0 - f).astype(BF16)
    v_ref[0] = mm(2 * width, width).astype(BF16)
    z_ref[0] = _silu(mm(3 * width, width)).astype(BF16)

    fox_out = (fq_ref, fk_ref, fv_ref, fz_ref)
    per_group = width // piece

    def fox_piece(n):
        grp, part = n // per_group, n % per_group
        cs = slice(part * piece, (part + 1) * piece)
        y = mm((4 + grp) * width + part * piece, piece)
        if grp == 0:
            y = y * fq_scale
        elif grp == 3:
            y = _silu(y)
        fox_out[grp][0, :, cs] = y.astype(BF16)

    n_sub = x_ref.shape[1] // T
    stride = n_sub * HEADS * piece // (4 * width)

    def after_head(h, sub):
        n = sub * HEADS + h
        if n % stride == stride - 1:
            fox_piece(n // stride)

    for sub in range(n_sub):
        _hgrn2_tile(slice(sub * T, (sub + 1) * T), q_ref, lf_ref, k_ref, v_ref, z_ref, g_ref, hg_ref, st_ref,
                    T=T, after_head=functools.partial(after_head, sub=sub))


def _inproj_ab(x, w, wb, wg, lb, bfb, g, *, tm, T, ck):
    B, S, D = x.shape
    width = (w.shape[1] + wb.shape[1]) // 8
    piece = max(MXU_COLS, 4 * width // ((tm // T) * HEADS))
    assert width % piece == 0 and ((tm // T) * HEADS * piece) % (4 * width) == 0
    wide = jax.ShapeDtypeStruct((B, S, width), BF16)
    tok = pl.BlockSpec((1, tm, width), lambda b, s: (b, s, 0))
    const = lambda shape: pl.BlockSpec(shape, lambda b, s: (0,) * len(shape))
    return pl.pallas_call(
        functools.partial(_inproj_ab_kernel, width=width, fq_scale=HEAD_DK ** -0.5 * LOG2E, T=T, piece=piece),
        grid=(B, S // tm),
        in_specs=[pl.BlockSpec((1, tm, D), lambda b, s: (b, s, 0)),
                  const(w.shape), const(wb.shape), const(wg.shape), const(lb.shape), const(bfb.shape),
                  const(g.shape)],
        out_specs=[tok, tok, tok, tok, tok,
                   pl.BlockSpec((1, tm // ck, GATE_ROWS, ck), lambda b, s: (b, s, 0, 0))],
        out_shape=[wide, wide, wide, wide, wide,
                   jax.ShapeDtypeStruct((B, S // ck, GATE_ROWS, ck), F32)],
        scratch_shapes=[pltpu.VMEM((GATE_ROWS, LANES), F32), pltpu.VMEM((HEADS, HEAD_DK, HEAD_DK), F32),
                        pltpu.VMEM((tm, D), BF16),
                        pltpu.VMEM((1, tm, width), BF16), pltpu.VMEM((1, tm, width), F32),
                        pltpu.VMEM((1, tm, width), BF16), pltpu.VMEM((1, tm, width), BF16),
                        pltpu.VMEM((1, tm, width), BF16)],
        compiler_params=_cparams(("parallel", "arbitrary")),
        name="inproj_ab_hgrn2",
    )(x, w, wb, wg, lb, bfb, g)


def _hgrn2_tile(rs, q_ref, lf_ref, k_ref, v_ref, z_ref, g_ref, o_ref, st_ref, *, T, after_head=None):
    L = HG_CHUNK
    BL = 2 * L
    row = lax.broadcasted_iota(jnp.int32, (T, T), 0)
    col = lax.broadcasted_iota(jnp.int32, (T, T), 1)
    rc, cc = row // L, col // L
    same_chunk = rc == cc
    same_chunk_causal = same_chunk & (col <= row)
    prev_chunk_in_block = (rc == cc + 1) & (rc % 2 == 1)
    tri = jnp.where(same_chunk_causal, 1.0, 0.0).astype(BF16)
    last = jnp.where(same_chunk, 1.0, 0.0).astype(BF16)
    odd_row = (lax.broadcasted_iota(jnp.int32, (T, HEAD_DK), 0) // L) % 2 == 1

    lf_all = lf_ref[0, rs, :]
    lf_hi = lf_all.astype(BF16)
    lf_lo = (lf_all - lf_hi.astype(F32)).astype(BF16)
    b_all = (jnp.dot(tri, lf_hi, preferred_element_type=F32)
             + jnp.dot(tri, lf_lo, preferred_element_type=F32))
    btot_all = (jnp.dot(last, lf_hi, preferred_element_type=F32)
                + jnp.dot(last, lf_lo, preferred_element_type=F32))

    for h in range(HEADS):
        sl = slice(h * HEAD_DK, (h + 1) * HEAD_DK)
        b = b_all[:, sl]
        btot = btot_all[:, sl]
        q = q_ref[0, rs, sl].astype(F32)
        k = k_ref[0, rs, sl].astype(F32)
        v = v_ref[0, rs, sl]
        dec_tot = jnp.exp(btot)
        dec_prev = pltpu.roll(dec_tot, L, 0)
        dec_next = pltpu.roll(dec_tot, T - L, 0)
        q_dec32 = q * jnp.exp(b)
        k_end32 = k * jnp.exp(btot - b)
        q_dec = q_dec32.astype(BF16)
        k_inv = (k * jnp.exp(-b)).astype(BF16)
        k_end = k_end32.astype(BF16)
        q_blk = jnp.where(odd_row, q_dec32 * dec_prev, q_dec32).astype(BF16)
        k_blk = jnp.where(odd_row, k_end32, k_end32 * dec_next).astype(BF16)
        dec_blk = dec_tot * dec_next

        nt = (((1,), (1,)), ((), ()))
        a_in = lax.dot_general(q_dec, k_inv, nt, preferred_element_type=F32)
        a_x = lax.dot_general(q_dec, k_end, nt, preferred_element_type=F32)
        a = jnp.where(same_chunk_causal, a_in, jnp.where(prev_chunk_in_block, a_x, 0.0)).astype(BF16)
        o = jnp.dot(a, v, preferred_element_type=F32)

        st = st_ref[h]
        inter = []
        for c in range(T // BL):
            bs = slice(c * BL, (c + 1) * BL)
            inter.append(lax.dot_general(q_blk[bs], st.astype(BF16), nt, preferred_element_type=F32))
            upd = lax.dot_general(v[bs], k_blk[bs], (((0,), (0,)), ((), ())),
                                  preferred_element_type=F32)
            st = st * dec_blk[c * BL:c * BL + 1, :] + upd
        st_ref[h] = st
        o = o + jnp.concatenate(inter, axis=0)

        ms = jnp.mean(o * o, axis=-1, keepdims=True)
        o = o * lax.rsqrt(ms + RMS_EPS) * g_ref[:, sl]
        o_ref[0, rs, sl] = (o * z_ref[0, rs, sl].astype(F32)).astype(BF16)
        if after_head is not None:
            after_head(h)


def _fox_kernel(q_ref, k_ref, v_ref, c_ref, z_ref, o_ref, vaug_ref, m_ref, acc_ref, *, tq, tk):
    h = pl.program_id(1)
    qi = pl.program_id(2)

    @pl.when(qi == 0)
    def _():
        vaug_ref[:, :HEAD_DK] = v_ref[0]
        vaug_ref[:, HEAD_DK:] = jnp.ones((vaug_ref.shape[0], HEAD_DK), BF16)

    per_q = tq // tk

    def c_row(j):
        return c_ref[0, j, pl.ds(h, 1), :] * LOG2E

    cref = c_row(qi * per_q)[:, :1]
    m_ref[...] = jnp.full_like(m_ref, -jnp.inf)
    acc_ref[...] = jnp.zeros_like(acc_ref)

    def chunk(j, r0, masked):
        ks = pl.multiple_of(j * tk, tk)
        s = lax.dot_general(q_ref[0, r0:, :], k_ref[0, pl.ds(ks, tk), :], (((1,), (1,)), ((), ())),
                            preferred_element_type=F32)
        s = s + (cref - c_row(j))
        if masked:
            row = lax.broadcasted_iota(jnp.int32, s.shape, 0)
            col = lax.broadcasted_iota(jnp.int32, s.shape, 1)
            s = jnp.where(col <= row, s, -jnp.inf)
        m_old = m_ref[r0:, :]
        m_new = jnp.maximum(m_old, jnp.max(s, axis=-1, keepdims=True))
        alpha = jnp.exp2(m_old - m_new)
        p = jnp.exp2(s - jnp.tile(m_new, (1, tk // LANES))).astype(BF16)
        acc_ref[r0:, :] = (jnp.tile(alpha, (1, 2)) * acc_ref[r0:, :]
                           + jnp.dot(p, vaug_ref[pl.ds(ks, tk), :], preferred_element_type=F32))
        m_ref[r0:, :] = m_new

    def body(i, carry):
        for u in range(per_q):
            chunk(i * per_q + u, 0, False)
        return carry

    lax.fori_loop(0, qi, body, 0)
    for u in range(per_q):
        chunk(qi * per_q + u, u * tk, True)
    acc = acc_ref[...]
    o = acc[:, :HEAD_DK] / acc[:, HEAD_DK:]
    o_ref[0] = (o * z_ref[0].astype(F32)).astype(BF16)


def _fox(q, k, v, c4, z, *, tq):
    B, S, W = q.shape
    nk, tk = c4.shape[1], c4.shape[3]
    nq = S // tq
    qspec = pl.BlockSpec((1, tq, HEAD_DK), lambda b, h, i: (b, i, h))
    kspec = pl.BlockSpec((1, S, HEAD_DK), lambda b, h, i: (b, 0, h))
    return pl.pallas_call(
        functools.partial(_fox_kernel, tq=tq, tk=tk),
        grid=(B, HEADS, nq),
        in_specs=[qspec, kspec, kspec,
                  pl.BlockSpec((1, nk, GATE_ROWS, tk), lambda b, h, i: (b, 0, 0, 0)),
                  qspec],
        out_specs=qspec,
        out_shape=jax.ShapeDtypeStruct((B, S, W), BF16),
        scratch_shapes=[pltpu.VMEM((S, 2 * HEAD_DK), BF16), pltpu.VMEM((tq, LANES), F32),
                        pltpu.VMEM((tq, 2 * HEAD_DK), F32)],
        compiler_params=_cparams(("parallel", "parallel", "arbitrary")),
        name="fox_attention",
    )(q, k, v, c4, z)


def _outproj_ln_kernel(*refs, n_act):
    act_refs = refs[:n_act]
    w_ref, x_ref, g_ref, b_ref, o_ref = refs[n_act:]
    y = ALPHA * x_ref[...]
    off = 0
    for a_ref in act_refs:
        wd = a_ref.shape[1]
        y = y + jnp.dot(a_ref[...], w_ref[off:off + wd, :], preferred_element_type=F32)
        off += wd
    mu = jnp.mean(y, axis=-1, keepdims=True)
    yc = y - mu
    var = jnp.mean(yc * yc, axis=-1, keepdims=True)
    o_ref[...] = yc * lax.rsqrt(var + LN_EPS) * g_ref[...] + b_ref[...]


def _outproj_ln(acts, w, x, g, b, *, tm):
    N, D = x.shape
    tok = lambda wd: pl.BlockSpec((tm, wd), lambda i: (i, 0))
    const = lambda shape: pl.BlockSpec(shape, lambda i: (0,) * len(shape))
    return pl.pallas_call(
        functools.partial(_outproj_ln_kernel, n_act=len(acts)),
        grid=(N // tm,),
        in_specs=[tok(a.shape[1]) for a in acts] + [const(w.shape), tok(D), const(g.shape), const(b.shape)],
        out_specs=tok(D),
        out_shape=jax.ShapeDtypeStruct((N, D), F32),
        compiler_params=_cparams(("parallel",)),
        name="outproj_ln",
    )(*acts, w, x, g, b)


def _inproj_c_kernel(x_ref, w_ref, wb_ref, wg_ref, cw_ref, cb_ref, gb_ref,
                     q_ref, k_ref, v_ref, og_ref, z_ref, gt_ref,
                     ubuf_ref, *, tm, kw, vw, k_scale):
    si = pl.program_id(1)
    xb = x_ref[0].astype(BF16)

    g = lax.dot_general(wg_ref[...], xb, (((1,), (1,)), ((), ())), preferred_element_type=F32) + gb_ref[...]
    rowi = lax.broadcasted_iota(jnp.int32, g.shape, 0)
    b = _cumsum_lanes(_log_sigmoid(g), seg=ML_CHUNK)
    gt_ref[0] = jnp.where(rowi < HEADS, g - pltpu.roll(b, HEADS, 0), b)

    def mm(lo, hi):
        return _mm_split(xb, w_ref, wb_ref, lo, hi - lo)

    pad = SUBLANES

    @pl.when(si == 0)
    def _():
        ubuf_ref[0:pad, :] = jnp.zeros((pad, 2 * kw), F32)

    cwid = MXU_COLS
    o0 = 2 * kw

    def conv_chunk(c0):
        cs = slice(c0, c0 + cwid)
        ubuf_ref[pad:pad + tm, cs] = mm(c0, c0 + cwid)
        acc = cb_ref[:, cs]
        for j in range(ML_CONV):
            st = pad - (ML_CONV - 1) + j
            acc = acc + cw_ref[j:j + 1, cs] * ubuf_ref[st:st + tm, cs]
        ubuf_ref[0:pad, cs] = ubuf_ref[tm:tm + pad, cs]
        if c0 < kw:
            q_ref[0, :, cs] = _silu(acc).astype(BF16)
        else:
            k_ref[0, :, c0 - kw:c0 - kw + cwid] = (_silu(acc) * k_scale).astype(BF16)

    assert 2 * kw == vw
    for c0 in range(0, vw, cwid):
        cs = slice(c0, c0 + cwid)
        conv_chunk(c0)
        v_ref[0, :, cs] = mm(o0 + c0, o0 + c0 + cwid).astype(BF16)
        og_ref[0, :, cs] = _sigmoid(mm(o0 + vw + c0, o0 + vw + c0 + cwid)).astype(BF16)
        z_ref[0, :, cs] = _silu(mm(o0 + 2 * vw + c0, o0 + 2 * vw + c0 + cwid)).astype(BF16)


def _inproj_c(x, w, wb, wg, cw, cb, gb, *, tm):
    B, S, D = x.shape
    kw = cw.shape[1] // 2
    vw = (w.shape[1] + wb.shape[1] - 2 * kw) // 3
    const = lambda shape: pl.BlockSpec(shape, lambda b, s: (0,) * len(shape))
    tok = lambda wd: pl.BlockSpec((1, tm, wd), lambda b, s: (b, s, 0))
    return pl.pallas_call(
        functools.partial(_inproj_c_kernel, tm=tm, kw=kw, vw=vw, k_scale=HEAD_DK ** -0.5),
        grid=(B, S // tm),
        in_specs=[pl.BlockSpec((1, tm, D), lambda b, s: (b, s, 0)),
                  const(w.shape), const(wb.shape), const(wg.shape), const(cw.shape), const(cb.shape),
                  const(gb.shape)],
        out_specs=[tok(kw), tok(kw), tok(vw), tok(vw), tok(vw),
                   pl.BlockSpec((1, GATE_ROWS, tm), lambda b, s: (b, 0, s))],
        out_shape=[jax.ShapeDtypeStruct((B, S, kw), BF16), jax.ShapeDtypeStruct((B, S, kw), BF16),
                   jax.ShapeDtypeStruct((B, S, vw), BF16), jax.ShapeDtypeStruct((B, S, vw), BF16),
                   jax.ShapeDtypeStruct((B, S, vw), BF16), jax.ShapeDtypeStruct((B, GATE_ROWS, S), F32)],
        scratch_shapes=[pltpu.VMEM((tm + SUBLANES, 2 * kw), F32)],
        compiler_params=_cparams(("parallel", "arbitrary")),
        name="inproj_c",
    )(x, w, wb, wg, cw, cb, gb)


def _mlstm_kernel(q_ref, k_ref, v_ref, gt_ref, og_ref, z_ref, g_ref, o_ref, c_ref, m_ref, *, L):
    @pl.when(pl.program_id(1) == 0)
    def _():
        c_ref[...] = jnp.zeros_like(c_ref)
        m_ref[...] = jnp.zeros_like(m_ref)

    for sub in range(q_ref.shape[1] // L):
        for h in range(HEADS):
            _mlstm_head(h, slice(sub * L, (sub + 1) * L),
                        q_ref, k_ref, v_ref, gt_ref, og_ref, z_ref, g_ref, o_ref, c_ref, m_ref, L=L)


def _mlstm_head(h, rs, q_ref, k_ref, v_ref, gt_ref, og_ref, z_ref, g_ref, o_ref, c_ref, m_ref, *, L):
    row = lax.broadcasted_iota(jnp.int32, (L, L), 0)
    col = lax.broadcasted_iota(jnp.int32, (L, L), 1)
    tril = col <= row
    eye = col == row
    ks = slice(h * HEAD_DK, (h + 1) * HEAD_DK)
    vs = slice(h * ML_DV, (h + 1) * ML_DV)
    q = q_ref[0, rs, ks]
    k = k_ref[0, rs, ks]
    v = v_ref[0, rs, vs]
    a_row = gt_ref[0, h:h + 1, rs]
    b_row = gt_ref[0, HEADS + h:HEADS + h + 1, rs]
    m_prev = m_ref[h:h + 1, :1]

    M_col = jnp.maximum(m_prev, jnp.max(jnp.where(tril, a_row, -jnp.inf), axis=-1, keepdims=True))
    b_col = jnp.sum(jnp.where(eye, b_row, 0.0), axis=-1, keepdims=True)
    w = jnp.where(tril, jnp.exp(a_row - M_col), 0.0)
    s_inter = jnp.exp(m_prev - M_col)

    v_ext = jnp.concatenate([v, jnp.ones((L, HEAD_DK), BF16)], axis=1)
    qk = lax.dot_general(q, k, (((1,), (1,)), ((), ())), preferred_element_type=F32) * w
    ne = (jnp.dot(qk.astype(BF16), v_ext, preferred_element_type=F32)
          + s_inter * jnp.dot(q, c_ref[h].astype(BF16), preferred_element_type=F32))
    den = jnp.maximum(jnp.abs(ne[:, ML_DV:]), jnp.exp(-(b_col + M_col)))
    hval = ne[:, :ML_DV] / jnp.tile(den, (1, ML_DV // HEAD_DK))

    M_last = jnp.maximum(m_prev, jnp.max(a_row, axis=-1, keepdims=True))
    wk_row = jnp.exp(a_row - M_last)
    decay = jnp.exp(m_prev - M_last)
    wk_col = jnp.sum(jnp.where(eye, wk_row, 0.0), axis=-1, keepdims=True)
    kw = (k.astype(F32) * wk_col).astype(BF16)
    c_ref[h] = decay * c_ref[h] + lax.dot_general(kw, v_ext, (((0,), (0,)), ((), ())),
                                                  preferred_element_type=F32)
    m_ref[h:h + 1, :] = jnp.broadcast_to(b_row[:, L - 1:] + M_last, (1, m_ref.shape[1]))

    ht = og_ref[0, rs, vs].astype(F32) * hval
    ms = jnp.mean(ht * ht, axis=-1, keepdims=True)
    ht = ht * lax.rsqrt(ms + RMS_EPS) * g_ref[:, vs]
    o_ref[0, rs, vs] = (ht * z_ref[0, rs, vs].astype(F32)).astype(BF16)


def _mlstm(q, k, v, gt, og, z, g, *, L, T):
    B, S, KW = q.shape
    W = v.shape[2]
    kspec = pl.BlockSpec((1, T, KW), lambda b, c: (b, c, 0))
    vspec = pl.BlockSpec((1, T, W), lambda b, c: (b, c, 0))
    return pl.pallas_call(
        functools.partial(_mlstm_kernel, L=L),
        grid=(B, S // T),
        in_specs=[kspec, kspec, vspec,
                  pl.BlockSpec((1, GATE_ROWS, T), lambda b, c: (b, 0, c)),
                  vspec, vspec, pl.BlockSpec((1, W), lambda b, c: (0, 0))],
        out_specs=vspec,
        out_shape=jax.ShapeDtypeStruct((B, S, W), BF16),
        scratch_shapes=[pltpu.VMEM((HEADS, HEAD_DK, ML_DV + HEAD_DK), F32),
                        pltpu.VMEM((GATE_ROWS, LANES), F32)],
        compiler_params=_cparams(("parallel", "arbitrary")),
        name="mlstm",
    )(q, k, v, gt, og, z, g)


def _gate_rows(w_in, start, n):
    wt = w_in[:, start:start + LANES].T[:n]
    return jnp.pad(wt, ((0, GATE_ROWS - n), (0, 0))).astype(BF16)


def _split_cast(w_in, start, n):
    return w_in[:, :start].astype(BF16), w_in[:, start + n:].astype(BF16)


def kernel(x, hgrn_lb_logits, ab_w_in, ab_fox_bf, ab_hgrn_norm_g, ab_w_out, c_w_in,
           c_conv_w, c_conv_b, c_bi, c_bf, c_norm_g, c_w_out, ln_g, ln_b):
    B, S, D = x.shape
    W = HEADS * HEAD_DK
    TM = min(PROJ_ROWS, S)

    lb_table = jnp.cumsum(jax.nn.softmax(hgrn_lb_logits.astype(F32), axis=0), axis=0)
    h = x.astype(F32)

    w_in = ab_w_in[0].astype(F32)
    w_a, w_b = _split_cast(w_in, 7 * W, HEADS)
    w_gate = _gate_rows(w_in, 7 * W, HEADS)
    bfb = jnp.broadcast_to(jnp.pad(ab_fox_bf[0].astype(F32), (0, GATE_ROWS - HEADS))[:, None], (GATE_ROWS, TM))
    hg, fq, fk, fv, fz, c = _inproj_ab(h, w_a, w_b, w_gate, lb_table[0][None, :], bfb,
                                       ab_hgrn_norm_g[0].astype(F32)[None, :],
                                       tm=TM, T=min(HG_ROWS, S), ck=min(FOX_K_ROWS, S))
    fx = _fox(fq, fk, fv, c, fz, tq=min(FOX_Q_ROWS, S))
    h = _outproj_ln([hg.reshape(B * S, W), fx.reshape(B * S, W)], ab_w_out[0].astype(BF16),
                    h.reshape(B * S, D), ln_g[0].astype(F32)[None, :], ln_b[0].astype(F32)[None, :],
                    tm=TM).reshape(B, S, D)

    w_in = c_w_in[0].astype(F32)
    KW, VW = W, HEADS * ML_DV
    o_i = 2 * KW + VW
    w_a, w_b = _split_cast(w_in, o_i, 2 * HEADS)
    w_gate = _gate_rows(w_in, o_i, 2 * HEADS)
    gb = jnp.broadcast_to(jnp.concatenate([c_bi[0], c_bf[0]]).astype(F32)[:, None], (GATE_ROWS, TM))
    mq, mk, mv, og, mz, gt = _inproj_c(h, w_a, w_b, w_gate, c_conv_w[0].astype(F32),
                                       c_conv_b[0].astype(F32)[None, :], gb, tm=TM)
    ht = _mlstm(mq, mk, mv, gt, og, mz, c_norm_g[0].astype(F32)[None, :], L=ML_CHUNK, T=min(ML_ROWS, S))
    h = _outproj_ln([ht.reshape(B * S, VW)], c_w_out[0].astype(BF16), h.reshape(B * S, D),
                    ln_g[1].astype(F32)[None, :], ln_b[1].astype(F32)[None, :], tm=TM).reshape(B, S, D)
    return h.astype(x.dtype)
```

```python
import functools
import math

import jax
import jax.numpy as jnp
from jax import lax
from jax.experimental import pallas as pl
from jax.experimental.pallas import tpu as pltpu

F32 = jnp.float32
BF16 = jnp.bfloat16

DEPTH = 2
ALPHA = (2 * DEPTH) ** 0.25
LN_EPS = 1e-5
RMS_EPS = 1e-6

HEADS = 4
HEAD_DK = 128
ML_DV = 256
HG_CHUNK = 32
ML_CONV = 4
ML_CHUNK = 256

LANES = 128
SUBLANES = 8
MXU_COLS = 256
V7X_VMEM_BYTES = 64 * 1024 * 1024
VMEM_LIMIT = V7X_VMEM_BYTES * 7 // 8

GATE_ROWS = SUBLANES

PROJ_ROWS = 1024
HG_ROWS = 256
FOX_Q_ROWS = 4096
FOX_K_ROWS = 256
ML_ROWS = 4 * ML_CHUNK

LOG2E = math.log2(math.e)


def _cparams(sem):
    return pltpu.CompilerParams(dimension_semantics=sem, vmem_limit_bytes=VMEM_LIMIT)


def _sigmoid(z):
    return 0.5 + 0.5 * jnp.tanh(0.5 * z)


def _silu(z):
    hz = 0.5 * z
    return hz + hz * jnp.tanh(hz)


def _log_sigmoid(z):
    return jnp.minimum(z, 0.0) - jnp.log(1.0 + jnp.exp(-jnp.abs(z)))


def _cumsum_lanes(x, seg=None):
    seg = seg or x.shape[-1]
    pos = lax.broadcasted_iota(jnp.int32, x.shape, x.ndim - 1) % seg
    sh = 1
    while sh < seg:
        x = x + jnp.where(pos >= sh, pltpu.roll(x, sh, x.ndim - 1), 0.0)
        sh *= 2
    return x


def _mm_split(x, w_ref, wb_ref, lo, n):
    na = w_ref.shape[1]
    if lo + n <= na:
        return jnp.dot(x, w_ref[:, lo:lo + n], preferred_element_type=F32)
    assert lo >= na
    return jnp.dot(x, wb_ref[:, lo - na:lo - na + n], preferred_element_type=F32)


def _inproj_ab_kernel(x_ref, w_ref, wb_ref, wg_ref, lb_ref, bf_ref, g_ref,
                      hg_ref, fq_ref, fk_ref, fv_ref, fz_ref, c_ref,
                      carry_ref, st_ref, xb_ref, q_ref, lf_ref, k_ref, v_ref, z_ref,
                      *, width, fq_scale, T, piece):
    si = pl.program_id(1)
    xb_ref[...] = x_ref[0].astype(BF16)
    xb = xb_ref[...]

    g = lax.dot_general(wg_ref[...], xb, (((1,), (1,)), ((), ())), preferred_element_type=F32)
    ls = _log_sigmoid(g + bf_ref[...])

    @pl.when(si == 0)
    def _():
        carry_ref[...] = jnp.zeros_like(carry_ref)
        st_ref[...] = jnp.zeros_like(st_ref)

    c = _cumsum_lanes(ls) + carry_ref[:, :1]
    ck = c_ref.shape[3]
    for j in range(c_ref.shape[1]):
        c_ref[0, j] = c[:, j * ck:(j + 1) * ck]
    carry_ref[...] = jnp.broadcast_to(c[:, -1:], carry_ref.shape)

    def mm(lo, n):
        return _mm_split(xb_ref[...], w_ref, wb_ref, lo, n)

    q_ref[0] = mm(0, width).astype(BF16)
    lb = lb_ref[...]
    f = lb + (1.0 - lb) * _sigmoid(mm(width, width))
    lf_ref[0] = jnp.log(f)
    k_ref[0] = (1.0 - f).astype(BF16)
    v_ref[0] = mm(2 * width, width).astype(BF16)
    z_ref[0] = _silu(mm(3 * width, width)).astype(BF16)

    fox_out = (fq_ref, fk_ref, fv_ref, fz_ref)
    per_group = width // piece

    def fox_piece(n):
        grp, part = n // per_group, n % per_group
        cs = slice(part * piece, (part + 1) * piece)
        y = mm((4 + grp) * width + part * piece, piece)
        if grp == 0:
            y = y * fq_scale
        elif grp == 3:
            y = _silu(y)
        fox_out[grp][0, :, cs] = y.astype(BF16)

    n_sub = x_ref.shape[1] // T
    stride = n_sub * HEADS * piece // (4 * width)

    def after_head(h, sub):
        n = sub * HEADS + h
        if n % stride == stride - 1:
            fox_piece(n // stride)

    for sub in range(n_sub):
        _hgrn2_tile(slice(sub * T, (sub + 1) * T), q_ref, lf_ref, k_ref, v_ref, z_ref, g_ref, hg_ref, st_ref,
                    T=T, after_head=functools.partial(after_head, sub=sub))


def _inproj_ab(x, w, wb, wg, lb, bfb, g, *, tm, T, ck):
    B, S, D = x.shape
    width = (w.shape[1] + wb.shape[1]) // 8
    piece = max(MXU_COLS, 4 * width // ((tm // T) * HEADS))
    assert width % piece == 0 and ((tm // T) * HEADS * piece) % (4 * width) == 0
    wide = jax.ShapeDtypeStruct((B, S, width), BF16)
    tok = pl.BlockSpec((1, tm, width), lambda b, s: (b, s, 0))
    const = lambda shape: pl.BlockSpec(shape, lambda b, s: (0,) * len(shape))
    return pl.pallas_call(
        functools.partial(_inproj_ab_kernel, width=width, fq_scale=HEAD_DK ** -0.5 * LOG2E, T=T, piece=piece),
        grid=(B, S // tm),
        in_specs=[pl.BlockSpec((1, tm, D), lambda b, s: (b, s, 0)),
                  const(w.shape), const(wb.shape), const(wg.shape), const(lb.shape), const(bfb.shape),
                  const(g.shape)],
        out_specs=[tok, tok, tok, tok, tok,
                   pl.BlockSpec((1, tm // ck, GATE_ROWS, ck), lambda b, s: (b, s, 0, 0))],
        out_shape=[wide, wide, wide, wide, wide,
                   jax.ShapeDtypeStruct((B, S // ck, GATE_ROWS, ck), F32)],
        scratch_shapes=[pltpu.VMEM((GATE_ROWS, LANES), F32), pltpu.VMEM((HEADS, HEAD_DK, HEAD_DK), F32),
                        pltpu.VMEM((tm, D), BF16),
                        pltpu.VMEM((1, tm, width), BF16), pltpu.VMEM((1, tm, width), F32),
                        pltpu.VMEM((1, tm, width), BF16), pltpu.VMEM((1, tm, width), BF16),
                        pltpu.VMEM((1, tm, width), BF16)],
        compiler_params=_cparams(("parallel", "arbitrary")),
        name="inproj_ab_hgrn2",
    )(x, w, wb, wg, lb, bfb, g)


def _hgrn2_tile(rs, q_ref, lf_ref, k_ref, v_ref, z_ref, g_ref, o_ref, st_ref, *, T, after_head=None):
    L = HG_CHUNK
    BL = 2 * L
    row = lax.broadcasted_iota(jnp.int32, (T, T), 0)
    col = lax.broadcasted_iota(jnp.int32, (T, T), 1)
    rc, cc = row // L, col // L
    same_chunk = rc == cc
    same_chunk_causal = same_chunk & (col <= row)
    prev_chunk_in_block = (rc == cc + 1) & (rc % 2 == 1)
    tri = jnp.where(same_chunk_causal, 1.0, 0.0).astype(BF16)
    last = jnp.where(same_chunk, 1.0, 0.0).astype(BF16)
    odd_row = (lax.broadcasted_iota(jnp.int32, (T, HEAD_DK), 0) // L) % 2 == 1

    lf_all = lf_ref[0, rs, :]
    lf_hi = lf_all.astype(BF16)
    lf_lo = (lf_all - lf_hi.astype(F32)).astype(BF16)
    b_all = (jnp.dot(tri, lf_hi, preferred_element_type=F32)
             + jnp.dot(tri, lf_lo, preferred_element_type=F32))
    btot_all = (jnp.dot(last, lf_hi, preferred_element_type=F32)
                + jnp.dot(last, lf_lo, preferred_element_type=F32))

    for h in range(HEADS):
        sl = slice(h * HEAD_DK, (h + 1) * HEAD_DK)
        b = b_all[:, sl]
        btot = btot_all[:, sl]
        q = q_ref[0, rs, sl].astype(F32)
        k = k_ref[0, rs, sl].astype(F32)
        v = v_ref[0, rs, sl]
        dec_tot = jnp.exp(btot)
        dec_prev = pltpu.roll(dec_tot, L, 0)
        dec_next = pltpu.roll(dec_tot, T - L, 0)
        q_dec32 = q * jnp.exp(b)
        k_end32 = k * jnp.exp(btot - b)
        q_dec = q_dec32.astype(BF16)
        k_inv = (k * jnp.exp(-b)).astype(BF16)
        k_end = k_end32.astype(BF16)
        q_blk = jnp.where(odd_row, q_dec32 * dec_prev, q_dec32).astype(BF16)
        k_blk = jnp.where(odd_row, k_end32, k_end32 * dec_next).astype(BF16)
        dec_blk = dec_tot * dec_next

        nt = (((1,), (1,)), ((), ()))
        a_in = lax.dot_general(q_dec, k_inv, nt, preferred_element_type=F32)
        a_x = lax.dot_general(q_dec, k_end, nt, preferred_element_type=F32)
        a = jnp.where(same_chunk_causal, a_in, jnp.where(prev_chunk_in_block, a_x, 0.0)).astype(BF16)
        o = jnp.dot(a, v, preferred_element_type=F32)

        st = st_ref[h]
        inter = []
        for c in range(T // BL):
            bs = slice(c * BL, (c + 1) * BL)
            inter.append(lax.dot_general(q_blk[bs], st.astype(BF16), nt, preferred_element_type=F32))
            upd = lax.dot_general(v[bs], k_blk[bs], (((0,), (0,)), ((), ())),
                                  preferred_element_type=F32)
            st = st * dec_blk[c * BL:c * BL + 1, :] + upd
        st_ref[h] = st
        o = o + jnp.concatenate(inter, axis=0)

        ms = jnp.mean(o * o, axis=-1, keepdims=True)
        o = o * lax.rsqrt(ms + RMS_EPS) * g_ref[:, sl]
        o_ref[0, rs, sl] = (o * z_ref[0, rs, sl].astype(F32)).astype(BF16)
        if after_head is not None:
            after_head(h)


def _fox_kernel(q_ref, k_ref, v_ref, c_ref, z_ref, o_ref, vaug_ref, m_ref, acc_ref, *, tq, tk):
    h = pl.program_id(1)
    qi = pl.program_id(2)

    @pl.when(qi == 0)
    def _():
        vaug_ref[:, :HEAD_DK] = v_ref[0]
        vaug_ref[:, HEAD_DK:] = jnp.ones((vaug_ref.shape[0], HEAD_DK), BF16)

    per_q = tq // tk

    def c_row(j):
        return c_ref[0, j, pl.ds(h, 1), :] * LOG2E

    cref = c_row(qi * per_q)[:, :1]
    m_ref[...] = jnp.full_like(m_ref, -jnp.inf)
    acc_ref[...] = jnp.zeros_like(acc_ref)

    def chunk(j, r0, masked):
        ks = pl.multiple_of(j * tk, tk)
        s = lax.dot_general(q_ref[0, r0:, :], k_ref[0, pl.ds(ks, tk), :], (((1,), (1,)), ((), ())),
                            preferred_element_type=F32)
        s = s + (cref - c_row(j))
        if masked:
            row = lax.broadcasted_iota(jnp.int32, s.shape, 0)
            col = lax.broadcasted_iota(jnp.int32, s.shape, 1)
            s = jnp.where(col <= row, s, -jnp.inf)
        m_old = m_ref[r0:, :]
        m_new = jnp.maximum(m_old, jnp.max(s, axis=-1, keepdims=True))
        alpha = jnp.exp2(m_old - m_new)
        p = jnp.exp2(s - jnp.tile(m_new, (1, tk // LANES))).astype(BF16)
        acc_ref[r0:, :] = (jnp.tile(alpha, (1, 2)) * acc_ref[r0:, :]
                           + jnp.dot(p, vaug_ref[pl.ds(ks, tk), :], preferred_element_type=F32))
        m_ref[r0:, :] = m_new

    def body(i, carry):
        for u in range(per_q):
            chunk(i * per_q + u, 0, False)
        return carry

    lax.fori_loop(0, qi, body, 0)
    for u in range(per_q):
        chunk(qi * per_q + u, u * tk, True)
    acc = acc_ref[...]
    o = acc[:, :HEAD_DK] / acc[:, HEAD_DK:]
    o_ref[0] = (o * z_ref[0].astype(F32)).astype(BF16)


def _fox(q, k, v, c4, z, *, tq):
    B, S, W = q.shape
    nk, tk = c4.shape[1], c4.shape[3]
    nq = S // tq
    qspec = pl.BlockSpec((1, tq, HEAD_DK), lambda b, h, i: (b, i, h))
    kspec = pl.BlockSpec((1, S, HEAD_DK), lambda b, h, i: (b, 0, h))
    return pl.pallas_call(
        functools.partial(_fox_kernel, tq=tq, tk=tk),
        grid=(B, HEADS, nq),
        in_specs=[qspec, kspec, kspec,
                  pl.BlockSpec((1, nk, GATE_ROWS, tk), lambda b, h, i: (b, 0, 0, 0)),
                  qspec],
        out_specs=qspec,
        out_shape=jax.ShapeDtypeStruct((B, S, W), BF16),
        scratch_shapes=[pltpu.VMEM((S, 2 * HEAD_DK), BF16), pltpu.VMEM((tq, LANES), F32),
                        pltpu.VMEM((tq, 2 * HEAD_DK), F32)],
        compiler_params=_cparams(("parallel", "parallel", "arbitrary")),
        name="fox_attention",
    )(q, k, v, c4, z)


def _outproj_ln_kernel(*refs, n_act):
    act_refs = refs[:n_act]
    w_ref, x_ref, g_ref, b_ref, o_ref = refs[n_act:]
    y = ALPHA * x_ref[...]
    off = 0
    for a_ref in act_refs:
        wd = a_ref.shape[1]
        y = y + jnp.dot(a_ref[...], w_ref[off:off + wd, :], preferred_element_type=F32)
        off += wd
    mu = jnp.mean(y, axis=-1, keepdims=True)
    yc = y - mu
    var = jnp.mean(yc * yc, axis=-1, keepdims=True)
    o_ref[...] = yc * lax.rsqrt(var + LN_EPS) * g_ref[...] + b_ref[...]


def _outproj_ln(acts, w, x, g, b, *, tm):
    N, D = x.shape
    tok = lambda wd: pl.BlockSpec((tm, wd), lambda i: (i, 0))
    const = lambda shape: pl.BlockSpec(shape, lambda i: (0,) * len(shape))
    return pl.pallas_call(
        functools.partial(_outproj_ln_kernel, n_act=len(acts)),
        grid=(N // tm,),
        in_specs=[tok(a.shape[1]) for a in acts] + [const(w.shape), tok(D), const(g.shape), const(b.shape)],
        out_specs=tok(D),
        out_shape=jax.ShapeDtypeStruct((N, D), F32),
        compiler_params=_cparams(("parallel",)),
        name="outproj_ln",
    )(*acts, w, x, g, b)


def _inproj_c_kernel(x_ref, w_ref, wb_ref, wg_ref, cw_ref, cb_ref, gb_ref,
                     q_ref, k_ref, v_ref, og_ref, z_ref, gt_ref,
                     ubuf_ref, *, tm, kw, vw, k_scale):
    si = pl.program_id(1)
    xb = x_ref[0].astype(BF16)

    g = lax.dot_general(wg_ref[...], xb, (((1,), (1,)), ((), ())), preferred_element_type=F32) + gb_ref[...]
    rowi = lax.broadcasted_iota(jnp.int32, g.shape, 0)
    b = _cumsum_lanes(_log_sigmoid(g), seg=ML_CHUNK)
    gt_ref[0] = jnp.where(rowi < HEADS, g - pltpu.roll(b, HEADS, 0), b)

    def mm(lo, hi):
        return _mm_split(xb, w_ref, wb_ref, lo, hi - lo)

    pad = SUBLANES

    @pl.when(si == 0)
    def _():
        ubuf_ref[0:pad, :] = jnp.zeros((pad, 2 * kw), F32)

    cwid = MXU_COLS
    o0 = 2 * kw

    def conv_chunk(c0):
        cs = slice(c0, c0 + cwid)
        ubuf_ref[pad:pad + tm, cs] = mm(c0, c0 + cwid)
        acc = cb_ref[:, cs]
        for j in range(ML_CONV):
            st = pad - (ML_CONV - 1) + j
            acc = acc + cw_ref[j:j + 1, cs] * ubuf_ref[st:st + tm, cs]
        ubuf_ref[0:pad, cs] = ubuf_ref[tm:tm + pad, cs]
        if c0 < kw:
            q_ref[0, :, cs] = _silu(acc).astype(BF16)
        else:
            k_ref[0, :, c0 - kw:c0 - kw + cwid] = (_silu(acc) * k_scale).astype(BF16)

    assert 2 * kw == vw
    for c0 in range(0, vw, cwid):
        cs = slice(c0, c0 + cwid)
        conv_chunk(c0)
        v_ref[0, :, cs] = mm(o0 + c0, o0 + c0 + cwid).astype(BF16)
        og_ref[0, :, cs] = _sigmoid(mm(o0 + vw + c0, o0 + vw + c0 + cwid)).astype(BF16)
        z_ref[0, :, cs] = _silu(mm(o0 + 2 * vw + c0, o0 + 2 * vw + c0 + cwid)).astype(BF16)


def _inproj_c(x, w, wb, wg, cw, cb, gb, *, tm):
    B, S, D = x.shape
    kw = cw.shape[1] // 2
    vw = (w.shape[1] + wb.shape[1] - 2 * kw) // 3
    const = lambda shape: pl.BlockSpec(shape, lambda b, s: (0,) * len(shape))
    tok = lambda wd: pl.BlockSpec((1, tm, wd), lambda b, s: (b, s, 0))
    return pl.pallas_call(
        functools.partial(_inproj_c_kernel, tm=tm, kw=kw, vw=vw, k_scale=HEAD_DK ** -0.5),
        grid=(B, S // tm),
        in_specs=[pl.BlockSpec((1, tm, D), lambda b, s: (b, s, 0)),
                  const(w.shape), const(wb.shape), const(wg.shape), const(cw.shape), const(cb.shape),
                  const(gb.shape)],
        out_specs=[tok(kw), tok(kw), tok(vw), tok(vw), tok(vw),
                   pl.BlockSpec((1, GATE_ROWS, tm), lambda b, s: (b, 0, s))],
        out_shape=[jax.ShapeDtypeStruct((B, S, kw), BF16), jax.ShapeDtypeStruct((B, S, kw), BF16),
                   jax.ShapeDtypeStruct((B, S, vw), BF16), jax.ShapeDtypeStruct((B, S, vw), BF16),
                   jax.ShapeDtypeStruct((B, S, vw), BF16), jax.ShapeDtypeStruct((B, GATE_ROWS, S), F32)],
        scratch_shapes=[pltpu.VMEM((tm + SUBLANES, 2 * kw), F32)],
        compiler_params=_cparams(("parallel", "arbitrary")),
        name="inproj_c",
    )(x, w, wb, wg, cw, cb, gb)


def _mlstm_kernel(q_ref, k_ref, v_ref, gt_ref, og_ref, z_ref, g_ref, o_ref, c_ref, m_ref, *, L):
    @pl.when(pl.program_id(1) == 0)
    def _():
        c_ref[...] = jnp.zeros_like(c_ref)
        m_ref[...] = jnp.zeros_like(m_ref)

    for sub in range(q_ref.shape[1] // L):
        for h in range(HEADS):
            _mlstm_head(h, slice(sub * L, (sub + 1) * L),
                        q_ref, k_ref, v_ref, gt_ref, og_ref, z_ref, g_ref, o_ref, c_ref, m_ref, L=L)


def _mlstm_head(h, rs, q_ref, k_ref, v_ref, gt_ref, og_ref, z_ref, g_ref, o_ref, c_ref, m_ref, *, L):
    row = lax.broadcasted_iota(jnp.int32, (L, L), 0)
    col = lax.broadcasted_iota(jnp.int32, (L, L), 1)
    tril = col <= row
    eye = col == row
    ks = slice(h * HEAD_DK, (h + 1) * HEAD_DK)
    vs = slice(h * ML_DV, (h + 1) * ML_DV)
    q = q_ref[0, rs, ks]
    k = k_ref[0, rs, ks]
    v = v_ref[0, rs, vs]
    a_row = gt_ref[0, h:h + 1, rs]
    b_row = gt_ref[0, HEADS + h:HEADS + h + 1, rs]
    m_prev = m_ref[h:h + 1, :1]

    M_col = jnp.maximum(m_prev, jnp.max(jnp.where(tril, a_row, -jnp.inf), axis=-1, keepdims=True))
    b_col = jnp.sum(jnp.where(eye, b_row, 0.0), axis=-1, keepdims=True)
    w = jnp.where(tril, jnp.exp(a_row - M_col), 0.0)
    s_inter = jnp.exp(m_prev - M_col)

    v_ext = jnp.concatenate([v, jnp.ones((L, HEAD_DK), BF16)], axis=1)
    qk = lax.dot_general(q, k, (((1,), (1,)), ((), ())), preferred_element_type=F32) * w
    ne = (jnp.dot(qk.astype(BF16), v_ext, preferred_element_type=F32)
          + s_inter * jnp.dot(q, c_ref[h].astype(BF16), preferred_element_type=F32))
    den = jnp.maximum(jnp.abs(ne[:, ML_DV:]), jnp.exp(-(b_col + M_col)))
    hval = ne[:, :ML_DV] / jnp.tile(den, (1, ML_DV // HEAD_DK))

    M_last = jnp.maximum(m_prev, jnp.max(a_row, axis=-1, keepdims=True))
    wk_row = jnp.exp(a_row - M_last)
    decay = jnp.exp(m_prev - M_last)
    wk_col = jnp.sum(jnp.where(eye, wk_row, 0.0), axis=-1, keepdims=True)
    kw = (k.astype(F32) * wk_col).astype(BF16)
    c_ref[h] = decay * c_ref[h] + lax.dot_general(kw, v_ext, (((0,), (0,)), ((), ())),
                                                  preferred_element_type=F32)
    m_ref[h:h + 1, :] = jnp.broadcast_to(b_row[:, L - 1:] + M_last, (1, m_ref.shape[1]))

    ht = og_ref[0, rs, vs].astype(F32) * hval
    ms = jnp.mean(ht * ht, axis=-1, keepdims=True)
    ht = ht * lax.rsqrt(ms + RMS_EPS) * g_ref[:, vs]
    o_ref[0, rs, vs] = (ht * z_ref[0, rs, vs].astype(F32)).astype(BF16)


def _mlstm(q, k, v, gt, og, z, g, *, L, T):
    B, S, KW = q.shape
    W = v.shape[2]
    kspec = pl.BlockSpec((1, T, KW), lambda b, c: (b, c, 0))
    vspec = pl.BlockSpec((1, T, W), lambda b, c: (b, c, 0))
    return pl.pallas_call(
        functools.partial(_mlstm_kernel, L=L),
        grid=(B, S // T),
        in_specs=[kspec, kspec, vspec,
                  pl.BlockSpec((1, GATE_ROWS, T), lambda b, c: (b, 0, c)),
                  vspec, vspec, pl.BlockSpec((1, W), lambda b, c: (0, 0))],
        out_specs=vspec,
        out_shape=jax.ShapeDtypeStruct((B, S, W), BF16),
        scratch_shapes=[pltpu.VMEM((HEADS, HEAD_DK, ML_DV + HEAD_DK), F32),
                        pltpu.VMEM((GATE_ROWS, LANES), F32)],
        compiler_params=_cparams(("parallel", "arbitrary")),
        name="mlstm",
    )(q, k, v, gt, og, z, g)


def _gate_rows(w_in, start, n):
    wt = w_in[:, start:start + LANES].T[:n]
    return jnp.pad(wt, ((0, GATE_ROWS - n), (0, 0))).astype(BF16)


def _split_cast(w_in, start, n):
    return w_in[:, :start].astype(BF16), w_in[:, start + n:].astype(BF16)


def kernel(x, hgrn_lb_logits, ab_w_in, ab_fox_bf, ab_hgrn_norm_g, ab_w_out, c_w_in,
           c_conv_w, c_conv_b, c_bi, c_bf, c_norm_g, c_w_out, ln_g, ln_b):
    B, S, D = x.shape
    W = HEADS * HEAD_DK
    TM = min(PROJ_ROWS, S)

    lb_table = jnp.cumsum(jax.nn.softmax(hgrn_lb_logits.astype(F32), axis=0), axis=0)
    h = x.astype(F32)

    w_in = ab_w_in[0].astype(F32)
    w_a, w_b = _split_cast(w_in, 7 * W, HEADS)
    w_gate = _gate_rows(w_in, 7 * W, HEADS)
    bfb = jnp.broadcast_to(jnp.pad(ab_fox_bf[0].astype(F32), (0, GATE_ROWS - HEADS))[:, None], (GATE_ROWS, TM))
    hg, fq, fk, fv, fz, c = _inproj_ab(h, w_a, w_b, w_gate, lb_table[0][None, :], bfb,
                                       ab_hgrn_norm_g[0].astype(F32)[None, :],
                                       tm=TM, T=min(HG_ROWS, S), ck=min(FOX_K_ROWS, S))
    fx = _fox(fq, fk, fv, c, fz, tq=min(FOX_Q_ROWS, S))
    h = _outproj_ln([hg.reshape(B * S, W), fx.reshape(B * S, W)], ab_w_out[0].astype(BF16),
                    h.reshape(B * S, D), ln_g[0].astype(F32)[None, :], ln_b[0].astype(F32)[None, :],
                    tm=TM).reshape(B, S, D)

    w_in = c_w_in[0].astype(F32)
    KW, VW = W, HEADS * ML_DV
    o_i = 2 * KW + VW
    w_a, w_b = _split_cast(w_in, o_i, 2 * HEADS)
    w_gate = _gate_rows(w_in, o_i, 2 * HEADS)
    gb = jnp.broadcast_to(jnp.concatenate([c_bi[0], c_bf[0]]).astype(F32)[:, None], (GATE_ROWS, TM))
    mq, mk, mv, og, mz, gt = _inproj_c(h, w_a, w_b, w_gate, c_conv_w[0].astype(F32),
                                       c_conv_b[0].astype(F32)[None, :], gb, tm=TM)
    ht = _mlstm(mq, mk, mv, gt, og, mz, c_norm_g[0].astype(F32)[None, :], L=ML_CHUNK, T=min(ML_ROWS, S))
    h = _outproj_ln([ht.reshape(B * S, VW)], c_w_out[0].astype(BF16), h.reshape(B * S, D),
                    ln_g[1].astype(F32)[None, :], ln_b[1].astype(F32)[None, :], tm=TM).reshape(B, S, D)
    return h.astype(x.dtype)
```

```python
import functools
import math

import jax
import jax.numpy as jnp
from jax import lax
from jax.experimental import pallas as pl
from jax.experimental.pallas import tpu as pltpu

F32 = jnp.float32
BF16 = jnp.bfloat16

DEPTH = 2
ALPHA = (2 * DEPTH) ** 0.25
LN_EPS = 1e-5
RMS_EPS = 1e-6

HEADS = 4
HEAD_DK = 128
ML_DV = 256
HG_CHUNK = 32
ML_CONV = 4
ML_CHUNK = 256

LANES = 128
SUBLANES = 8
MXU_COLS = 256
V7X_VMEM_BYTES = 64 * 1024 * 1024
VMEM_LIMIT = V7X_VMEM_BYTES * 7 // 8

GATE_ROWS = SUBLANES

PROJ_ROWS = 1024
HG_ROWS = 256
FOX_Q_ROWS = 4096
FOX_K_ROWS = 256
ML_ROWS = 8 * ML_CHUNK

LOG2E = math.log2(math.e)


def _cparams(sem):
    return pltpu.CompilerParams(dimension_semantics=sem, vmem_limit_bytes=VMEM_LIMIT)


def _sigmoid(z):
    return 0.5 + 0.5 * jnp.tanh(0.5 * z)


def _silu(z):
    hz = 0.5 * z
    return hz + hz * jnp.tanh(hz)


def _log_sigmoid(z):
    return jnp.minimum(z, 0.0) - jnp.log(1.0 + jnp.exp(-jnp.abs(z)))


def _cumsum_lanes(x, seg=None):
    seg = seg or x.shape[-1]
    pos = lax.broadcasted_iota(jnp.int32, x.shape, x.ndim - 1) % seg
    sh = 1
    while sh < seg:
        x = x + jnp.where(pos >= sh, pltpu.roll(x, sh, x.ndim - 1), 0.0)
        sh *= 2
    return x


def _mm_split(x, w_ref, wb_ref, lo, n):
    na = w_ref.shape[1]
    if lo + n <= na:
        return jnp.dot(x, w_ref[:, lo:lo + n], preferred_element_type=F32)
    assert lo >= na
    return jnp.dot(x, wb_ref[:, lo - na:lo - na + n], preferred_element_type=F32)


def _inproj_ab_kernel(x_ref, w_ref, wb_ref, wg_ref, lb_ref, bf_ref, g_ref,
                      hg_ref, fq_ref, fk_ref, fv_ref, fz_ref, c_ref,
                      carry_ref, st_ref, xb_ref, q_ref, lf_ref, k_ref, v_ref, z_ref,
                      *, width, fq_scale, T, piece):
    si = pl.program_id(1)
    xb_ref[...] = x_ref[0].astype(BF16)
    xb = xb_ref[...]

    g = lax.dot_general(wg_ref[...], xb, (((1,), (1,)), ((), ())), preferred_element_type=F32)
    ls = _log_sigmoid(g + bf_ref[...])

    @pl.when(si == 0)
    def _():
        carry_ref[...] = jnp.zeros_like(carry_ref)
        st_ref[...] = jnp.zeros_like(st_ref)

    c = _cumsum_lanes(ls) + carry_ref[:, :1]
    ck = c_ref.shape[3]
    for j in range(c_ref.shape[1]):
        c_ref[0, j] = c[:, j * ck:(j + 1) * ck]
    carry_ref[...] = jnp.broadcast_to(c[:, -1:], carry_ref.shape)

    def mm(lo, n):
        return _mm_split(xb_ref[...], w_ref, wb_ref, lo, n)

    q_ref[0] = mm(0, width).astype(BF16)
    lb = lb_ref[...]
    f = lb + (1.0 - lb) * _sigmoid(mm(width, width))
    lf_ref[0] = jnp.log(f)
    k_ref[0] = (1.0 - f).astype(BF16)
    v_ref[0] = mm(2 * width, width).astype(BF16)
    z_ref[0] = _silu(mm(3 * width, width)).astype(BF16)

    fox_out = (fq_ref, fk_ref, fv_ref, fz_ref)
    per_group = width // piece

    def fox_piece(n):
        grp, part = n // per_group, n % per_group
        cs = slice(part * piece, (part + 1) * piece)
        y = mm((4 + grp) * width + part * piece, piece)
        if grp == 0:
            y = y * fq_scale
        elif grp == 3:
            y = _silu(y)
        fox_out[grp][0, :, cs] = y.astype(BF16)

    n_sub = x_ref.shape[1] // T
    stride = n_sub * HEADS * piece // (4 * width)

    def after_head(h, sub):
        n = sub * HEADS + h
        if n % stride == stride - 1:
            fox_piece(n // stride)

    for sub in range(n_sub):
        _hgrn2_tile(slice(sub * T, (sub + 1) * T), q_ref, lf_ref, k_ref, v_ref, z_ref, g_ref, hg_ref, st_ref,
                    T=T, after_head=functools.partial(after_head, sub=sub))


def _inproj_ab(x, w, wb, wg, lb, bfb, g, *, tm, T, ck):
    B, S, D = x.shape
    width = (w.shape[1] + wb.shape[1]) // 8
    piece = max(MXU_COLS, 4 * width // ((tm // T) * HEADS))
    assert width % piece == 0 and ((tm // T) * HEADS * piece) % (4 * width) == 0
    wide = jax.ShapeDtypeStruct((B, S, width), BF16)
    tok = pl.BlockSpec((1, tm, width), lambda b, s: (b, s, 0))
    const = lambda shape: pl.BlockSpec(shape, lambda b, s: (0,) * len(shape))
    return pl.pallas_call(
        functools.partial(_inproj_ab_kernel, width=width, fq_scale=HEAD_DK ** -0.5 * LOG2E, T=T, piece=piece),
        grid=(B, S // tm),
        in_specs=[pl.BlockSpec((1, tm, D), lambda b, s: (b, s, 0)),
                  const(w.shape), const(wb.shape), const(wg.shape), const(lb.shape), const(bfb.shape),
                  const(g.shape)],
        out_specs=[tok, tok, tok, tok, tok,
                   pl.BlockSpec((1, tm // ck, GATE_ROWS, ck), lambda b, s: (b, s, 0, 0))],
        out_shape=[wide, wide, wide, wide, wide,
                   jax.ShapeDtypeStruct((B, S // ck, GATE_ROWS, ck), F32)],
        scratch_shapes=[pltpu.VMEM((GATE_ROWS, LANES), F32), pltpu.VMEM((HEADS, HEAD_DK, HEAD_DK), F32),
                        pltpu.VMEM((tm, D), BF16),
                        pltpu.VMEM((1, tm, width), BF16), pltpu.VMEM((1, tm, width), F32),
                        pltpu.VMEM((1, tm, width), BF16), pltpu.VMEM((1, tm, width), BF16),
                        pltpu.VMEM((1, tm, width), BF16)],
        compiler_params=_cparams(("parallel", "arbitrary")),
        name="inproj_ab_hgrn2",
    )(x, w, wb, wg, lb, bfb, g)


def _hgrn2_tile(rs, q_ref, lf_ref, k_ref, v_ref, z_ref, g_ref, o_ref, st_ref, *, T, after_head=None):
    L = HG_CHUNK
    BL = 2 * L
    row = lax.broadcasted_iota(jnp.int32, (T, T), 0)
    col = lax.broadcasted_iota(jnp.int32, (T, T), 1)
    rc, cc = row // L, col // L
    same_chunk = rc == cc
    same_chunk_causal = same_chunk & (col <= row)
    prev_chunk_in_block = (rc == cc + 1) & (rc % 2 == 1)
    tri = jnp.where(same_chunk_causal, 1.0, 0.0).astype(BF16)
    last = jnp.where(same_chunk, 1.0, 0.0).astype(BF16)
    odd_row = (lax.broadcasted_iota(jnp.int32, (T, HEAD_DK), 0) // L) % 2 == 1

    lf_all = lf_ref[0, rs, :]
    lf_hi = lf_all.astype(BF16)
    lf_lo = (lf_all - lf_hi.astype(F32)).astype(BF16)
    b_all = (jnp.dot(tri, lf_hi, preferred_element_type=F32)
             + jnp.dot(tri, lf_lo, preferred_element_type=F32))
    btot_all = (jnp.dot(last, lf_hi, preferred_element_type=F32)
                + jnp.dot(last, lf_lo, preferred_element_type=F32))

    for h in range(HEADS):
        sl = slice(h * HEAD_DK, (h + 1) * HEAD_DK)
        b = b_all[:, sl]
        btot = btot_all[:, sl]
        q = q_ref[0, rs, sl].astype(F32)
        k = k_ref[0, rs, sl].astype(F32)
        v = v_ref[0, rs, sl]
        dec_tot = jnp.exp(btot)
        dec_prev = pltpu.roll(dec_tot, L, 0)
        dec_next = pltpu.roll(dec_tot, T - L, 0)
        q_dec32 = q * jnp.exp(b)
        k_end32 = k * jnp.exp(btot - b)
        q_dec = q_dec32.astype(BF16)
        k_inv = (k * jnp.exp(-b)).astype(BF16)
        k_end = k_end32.astype(BF16)
        q_blk = jnp.where(odd_row, q_dec32 * dec_prev, q_dec32).astype(BF16)
        k_blk = jnp.where(odd_row, k_end32, k_end32 * dec_next).astype(BF16)
        dec_blk = dec_tot * dec_next

        nt = (((1,), (1,)), ((), ()))
        a_in = lax.dot_general(q_dec, k_inv, nt, preferred_element_type=F32)
        a_x = lax.dot_general(q_dec, k_end, nt, preferred_element_type=F32)
        a = jnp.where(same_chunk_causal, a_in, jnp.where(prev_chunk_in_block, a_x, 0.0)).astype(BF16)
        o = jnp.dot(a, v, preferred_element_type=F32)

        st = st_ref[h]
        inter = []
        for c in range(T // BL):
            bs = slice(c * BL, (c + 1) * BL)
            inter.append(lax.dot_general(q_blk[bs], st.astype(BF16), nt, preferred_element_type=F32))
            upd = lax.dot_general(v[bs], k_blk[bs], (((0,), (0,)), ((), ())),
                                  preferred_element_type=F32)
            st = st * dec_blk[c * BL:c * BL + 1, :] + upd
        st_ref[h] = st
        o = o + jnp.concatenate(inter, axis=0)

        ms = jnp.mean(o * o, axis=-1, keepdims=True)
        o = o * lax.rsqrt(ms + RMS_EPS) * g_ref[:, sl]
        o_ref[0, rs, sl] = (o * z_ref[0, rs, sl].astype(F32)).astype(BF16)
        if after_head is not None:
            after_head(h)


def _fox_kernel(q_ref, k_ref, v_ref, c_ref, z_ref, o_ref, vaug_ref, m_ref, acc_ref, *, tq, tk):
    h = pl.program_id(1)
    qi = pl.program_id(2)

    @pl.when(qi == 0)
    def _():
        vaug_ref[:, :HEAD_DK] = v_ref[0]
        vaug_ref[:, HEAD_DK:] = jnp.ones((vaug_ref.shape[0], HEAD_DK), BF16)

    per_q = tq // tk

    def c_row(j):
        return c_ref[0, j, pl.ds(h, 1), :] * LOG2E

    cref = c_row(qi * per_q)[:, :1]
    m_ref[...] = jnp.full_like(m_ref, -jnp.inf)
    acc_ref[...] = jnp.zeros_like(acc_ref)

    def chunk(j, r0, masked):
        ks = pl.multiple_of(j * tk, tk)
        s = lax.dot_general(q_ref[0, r0:, :], k_ref[0, pl.ds(ks, tk), :], (((1,), (1,)), ((), ())),
                            preferred_element_type=F32)
        s = s + (cref - c_row(j))
        if masked:
            row = lax.broadcasted_iota(jnp.int32, s.shape, 0)
            col = lax.broadcasted_iota(jnp.int32, s.shape, 1)
            s = jnp.where(col <= row, s, -jnp.inf)
        m_old = m_ref[r0:, :]
        m_new = jnp.maximum(m_old, jnp.max(s, axis=-1, keepdims=True))
        alpha = jnp.exp2(m_old - m_new)
        p = jnp.exp2(s - jnp.tile(m_new, (1, tk // LANES))).astype(BF16)
        acc_ref[r0:, :] = (jnp.tile(alpha, (1, 2)) * acc_ref[r0:, :]
                           + jnp.dot(p, vaug_ref[pl.ds(ks, tk), :], preferred_element_type=F32))
        m_ref[r0:, :] = m_new

    def body(i, carry):
        for u in range(per_q):
            chunk(i * per_q + u, 0, False)
        return carry

    lax.fori_loop(0, qi, body, 0)
    for u in range(per_q):
        chunk(qi * per_q + u, u * tk, True)
    acc = acc_ref[...]
    o = acc[:, :HEAD_DK] / acc[:, HEAD_DK:]
    o_ref[0] = (o * z_ref[0].astype(F32)).astype(BF16)


def _fox(q, k, v, c4, z, *, tq):
    B, S, W = q.shape
    nk, tk = c4.shape[1], c4.shape[3]
    nq = S // tq
    qspec = pl.BlockSpec((1, tq, HEAD_DK), lambda b, h, i: (b, i, h))
    kspec = pl.BlockSpec((1, S, HEAD_DK), lambda b, h, i: (b, 0, h))
    return pl.pallas_call(
        functools.partial(_fox_kernel, tq=tq, tk=tk),
        grid=(B, HEADS, nq),
        in_specs=[qspec, kspec, kspec,
                  pl.BlockSpec((1, nk, GATE_ROWS, tk), lambda b, h, i: (b, 0, 0, 0)),
                  qspec],
        out_specs=qspec,
        out_shape=jax.ShapeDtypeStruct((B, S, W), BF16),
        scratch_shapes=[pltpu.VMEM((S, 2 * HEAD_DK), BF16), pltpu.VMEM((tq, LANES), F32),
                        pltpu.VMEM((tq, 2 * HEAD_DK), F32)],
        compiler_params=_cparams(("parallel", "parallel", "arbitrary")),
        name="fox_attention",
    )(q, k, v, c4, z)


def _outproj_ln_kernel(*refs, n_act):
    act_refs = refs[:n_act]
    w_ref, x_ref, g_ref, b_ref, o_ref = refs[n_act:]
    y = ALPHA * x_ref[...]
    off = 0
    for a_ref in act_refs:
        wd = a_ref.shape[1]
        y = y + jnp.dot(a_ref[...], w_ref[off:off + wd, :], preferred_element_type=F32)
        off += wd
    mu = jnp.mean(y, axis=-1, keepdims=True)
    yc = y - mu
    var = jnp.mean(yc * yc, axis=-1, keepdims=True)
    o_ref[...] = yc * lax.rsqrt(var + LN_EPS) * g_ref[...] + b_ref[...]


def _outproj_ln(acts, w, x, g, b, *, tm):
    N, D = x.shape
    tok = lambda wd: pl.BlockSpec((tm, wd), lambda i: (i, 0))
    const = lambda shape: pl.BlockSpec(shape, lambda i: (0,) * len(shape))
    return pl.pallas_call(
        functools.partial(_outproj_ln_kernel, n_act=len(acts)),
        grid=(N // tm,),
        in_specs=[tok(a.shape[1]) for a in acts] + [const(w.shape), tok(D), const(g.shape), const(b.shape)],
        out_specs=tok(D),
        out_shape=jax.ShapeDtypeStruct((N, D), F32),
        compiler_params=_cparams(("parallel",)),
        name="outproj_ln",
    )(*acts, w, x, g, b)


def _inproj_c_kernel(x_ref, w_ref, wb_ref, wg_ref, cw_ref, cb_ref, gb_ref,
                     q_ref, k_ref, v_ref, og_ref, z_ref, gt_ref,
                     ubuf_ref, *, tm, kw, vw, k_scale):
    si = pl.program_id(1)
    xb = x_ref[0].astype(BF16)

    g = lax.dot_general(wg_ref[...], xb, (((1,), (1,)), ((), ())), preferred_element_type=F32) + gb_ref[...]
    rowi = lax.broadcasted_iota(jnp.int32, g.shape, 0)
    b = _cumsum_lanes(_log_sigmoid(g), seg=ML_CHUNK)
    gt_ref[0] = jnp.where(rowi < HEADS, g - pltpu.roll(b, HEADS, 0), b)

    def mm(lo, hi):
        return _mm_split(xb, w_ref, wb_ref, lo, hi - lo)

    pad = SUBLANES

    @pl.when(si == 0)
    def _():
        ubuf_ref[0:pad, :] = jnp.zeros((pad, 2 * kw), F32)

    cwid = MXU_COLS
    o0 = 2 * kw

    def conv_chunk(c0):
        cs = slice(c0, c0 + cwid)
        ubuf_ref[pad:pad + tm, cs] = mm(c0, c0 + cwid)
        acc = cb_ref[:, cs]
        for j in range(ML_CONV):
            st = pad - (ML_CONV - 1) + j
            acc = acc + cw_ref[j:j + 1, cs] * ubuf_ref[st:st + tm, cs]
        ubuf_ref[0:pad, cs] = ubuf_ref[tm:tm + pad, cs]
        if c0 < kw:
            q_ref[0, :, cs] = _silu(acc).astype(BF16)
        else:
            k_ref[0, :, c0 - kw:c0 - kw + cwid] = (_silu(acc) * k_scale).astype(BF16)

    assert 2 * kw == vw
    for c0 in range(0, vw, cwid):
        cs = slice(c0, c0 + cwid)
        conv_chunk(c0)
        v_ref[0, :, cs] = mm(o0 + c0, o0 + c0 + cwid).astype(BF16)
        og_ref[0, :, cs] = _sigmoid(mm(o0 + vw + c0, o0 + vw + c0 + cwid)).astype(BF16)
        z_ref[0, :, cs] = _silu(mm(o0 + 2 * vw + c0, o0 + 2 * vw + c0 + cwid)).astype(BF16)


def _inproj_c(x, w, wb, wg, cw, cb, gb, *, tm):
    B, S, D = x.shape
    kw = cw.shape[1] // 2
    vw = (w.shape[1] + wb.shape[1] - 2 * kw) // 3
    const = lambda shape: pl.BlockSpec(shape, lambda b, s: (0,) * len(shape))
    tok = lambda wd: pl.BlockSpec((1, tm, wd), lambda b, s: (b, s, 0))
    return pl.pallas_call(
        functools.partial(_inproj_c_kernel, tm=tm, kw=kw, vw=vw, k_scale=HEAD_DK ** -0.5),
        grid=(B, S // tm),
        in_specs=[pl.BlockSpec((1, tm, D), lambda b, s: (b, s, 0)),
                  const(w.shape), const(wb.shape), const(wg.shape), const(cw.shape), const(cb.shape),
                  const(gb.shape)],
        out_specs=[tok(kw), tok(kw), tok(vw), tok(vw), tok(vw),
                   pl.BlockSpec((1, GATE_ROWS, tm), lambda b, s: (b, 0, s))],
        out_shape=[jax.ShapeDtypeStruct((B, S, kw), BF16), jax.ShapeDtypeStruct((B, S, kw), BF16),
                   jax.ShapeDtypeStruct((B, S, vw), BF16), jax.ShapeDtypeStruct((B, S, vw), BF16),
                   jax.ShapeDtypeStruct((B, S, vw), BF16), jax.ShapeDtypeStruct((B, GATE_ROWS, S), F32)],
        scratch_shapes=[pltpu.VMEM((tm + SUBLANES, 2 * kw), F32)],
        compiler_params=_cparams(("parallel", "arbitrary")),
        name="inproj_c",
    )(x, w, wb, wg, cw, cb, gb)


def _mlstm_kernel(q_ref, k_ref, v_ref, gt_ref, og_ref, z_ref, g_ref, o_ref, c_ref, m_ref, *, L):
    @pl.when(pl.program_id(1) == 0)
    def _():
        c_ref[...] = jnp.zeros_like(c_ref)
        m_ref[...] = jnp.zeros_like(m_ref)

    for sub in range(q_ref.shape[1] // L):
        for h in range(HEADS):
            _mlstm_head(h, slice(sub * L, (sub + 1) * L),
                        q_ref, k_ref, v_ref, gt_ref, og_ref, z_ref, g_ref, o_ref, c_ref, m_ref, L=L)


def _mlstm_head(h, rs, q_ref, k_ref, v_ref, gt_ref, og_ref, z_ref, g_ref, o_ref, c_ref, m_ref, *, L):
    row = lax.broadcasted_iota(jnp.int32, (L, L), 0)
    col = lax.broadcasted_iota(jnp.int32, (L, L), 1)
    tril = col <= row
    eye = col == row
    ks = slice(h * HEAD_DK, (h + 1) * HEAD_DK)
    vs = slice(h * ML_DV, (h + 1) * ML_DV)
    q = q_ref[0, rs, ks]
    k = k_ref[0, rs, ks]
    v = v_ref[0, rs, vs]
    a_row = gt_ref[0, h:h + 1, rs]
    b_row = gt_ref[0, HEADS + h:HEADS + h + 1, rs]
    m_prev = m_ref[h:h + 1, :1]

    M_col = jnp.maximum(m_prev, jnp.max(jnp.where(tril, a_row, -jnp.inf), axis=-1, keepdims=True))
    b_col = jnp.sum(jnp.where(eye, b_row, 0.0), axis=-1, keepdims=True)
    w = jnp.where(tril, jnp.exp(a_row - M_col), 0.0)
    s_inter = jnp.exp(m_prev - M_col)

    v_ext = jnp.concatenate([v, jnp.ones((L, HEAD_DK), BF16)], axis=1)
    qk = lax.dot_general(q, k, (((1,), (1,)), ((), ())), preferred_element_type=F32) * w
    ne = (jnp.dot(qk.astype(BF16), v_ext, preferred_element_type=F32)
          + s_inter * jnp.dot(q, c_ref[h].astype(BF16), preferred_element_type=F32))
    den = jnp.maximum(jnp.abs(ne[:, ML_DV:]), jnp.exp(-(b_col + M_col)))
    hval = ne[:, :ML_DV] / jnp.tile(den, (1, ML_DV // HEAD_DK))

    M_last = jnp.maximum(m_prev, jnp.max(a_row, axis=-1, keepdims=True))
    wk_row = jnp.exp(a_row - M_last)
    decay = jnp.exp(m_prev - M_last)
    wk_col = jnp.sum(jnp.where(eye, wk_row, 0.0), axis=-1, keepdims=True)
    kw = (k.astype(F32) * wk_col).astype(BF16)
    c_ref[h] = decay * c_ref[h] + lax.dot_general(kw, v_ext, (((0,), (0,)), ((), ())),
                                                  preferred_element_type=F32)
    m_ref[h:h + 1, :] = jnp.broadcast_to(b_row[:, L - 1:] + M_last, (1, m_ref.shape[1]))

    ht = og_ref[0, rs, vs].astype(F32) * hval
    ms = jnp.mean(ht * ht, axis=-1, keepdims=True)
    ht = ht * lax.rsqrt(ms + RMS_EPS) * g_ref[:, vs]
    o_ref[0, rs, vs] = (ht * z_ref[0, rs, vs].astype(F32)).astype(BF16)


def _mlstm(q, k, v, gt, og, z, g, *, L, T):
    B, S, KW = q.shape
    W = v.shape[2]
    kspec = pl.BlockSpec((1, T, KW), lambda b, c: (b, c, 0))
    vspec = pl.BlockSpec((1, T, W), lambda b, c: (b, c, 0))
    return pl.pallas_call(
        functools.partial(_mlstm_kernel, L=L),
        grid=(B, S // T),
        in_specs=[kspec, kspec, vspec,
                  pl.BlockSpec((1, GATE_ROWS, T), lambda b, c: (b, 0, c)),
                  vspec, vspec, pl.BlockSpec((1, W), lambda b, c: (0, 0))],
        out_specs=vspec,
        out_shape=jax.ShapeDtypeStruct((B, S, W), BF16),
        scratch_shapes=[pltpu.VMEM((HEADS, HEAD_DK, ML_DV + HEAD_DK), F32),
                        pltpu.VMEM((GATE_ROWS, LANES), F32)],
        compiler_params=_cparams(("parallel", "arbitrary")),
        name="mlstm",
    )(q, k, v, gt, og, z, g)


def _gate_rows(w_in, start, n):
    wt = w_in[:, start:start + LANES].T[:n]
    return jnp.pad(wt, ((0, GATE_ROWS - n), (0, 0))).astype(BF16)


def _split_cast(w_in, start, n):
    return w_in[:, :start].astype(BF16), w_in[:, start + n:].astype(BF16)


def kernel(x, hgrn_lb_logits, ab_w_in, ab_fox_bf, ab_hgrn_norm_g, ab_w_out, c_w_in,
           c_conv_w, c_conv_b, c_bi, c_bf, c_norm_g, c_w_out, ln_g, ln_b):
    B, S, D = x.shape
    W = HEADS * HEAD_DK
    TM = min(PROJ_ROWS, S)

    lb_table = jnp.cumsum(jax.nn.softmax(hgrn_lb_logits.astype(F32), axis=0), axis=0)
    h = x.astype(F32)

    w_in = ab_w_in[0].astype(F32)
    w_a, w_b = _split_cast(w_in, 7 * W, HEADS)
    w_gate = _gate_rows(w_in, 7 * W, HEADS)
    bfb = jnp.broadcast_to(jnp.pad(ab_fox_bf[0].astype(F32), (0, GATE_ROWS - HEADS))[:, None], (GATE_ROWS, TM))
    hg, fq, fk, fv, fz, c = _inproj_ab(h, w_a, w_b, w_gate, lb_table[0][None, :], bfb,
                                       ab_hgrn_norm_g[0].astype(F32)[None, :],
                                       tm=TM, T=min(HG_ROWS, S), ck=min(FOX_K_ROWS, S))
    fx = _fox(fq, fk, fv, c, fz, tq=min(FOX_Q_ROWS, S))
    h = _outproj_ln([hg.reshape(B * S, W), fx.reshape(B * S, W)], ab_w_out[0].astype(BF16),
                    h.reshape(B * S, D), ln_g[0].astype(F32)[None, :], ln_b[0].astype(F32)[None, :],
                    tm=TM).reshape(B, S, D)

    w_in = c_w_in[0].astype(F32)
    KW, VW = W, HEADS * ML_DV
    o_i = 2 * KW + VW
    w_a, w_b = _split_cast(w_in, o_i, 2 * HEADS)
    w_gate = _gate_rows(w_in, o_i, 2 * HEADS)
    gb = jnp.broadcast_to(jnp.concatenate([c_bi[0], c_bf[0]]).astype(F32)[:, None], (GATE_ROWS, TM))
    mq, mk, mv, og, mz, gt = _inproj_c(h, w_a, w_b, w_gate, c_conv_w[0].astype(F32),
                                       c_conv_b[0].astype(F32)[None, :], gb, tm=TM)
    ht = _mlstm(mq, mk, mv, gt, og, mz, c_norm_g[0].astype(F32)[None, :], L=ML_CHUNK, T=min(ML_ROWS, S))
    h = _outproj_ln([ht.reshape(B * S, VW)], c_w_out[0].astype(BF16), h.reshape(B * S, D),
                    ln_g[1].astype(F32)[None, :], ln_b[1].astype(F32)[None, :], tm=TM).reshape(B, S, D)
    return h.astype(x.dtype)
```
